```python
import jax, jax.numpy as jnp
from jax import lax
import numpy as np

D_MODEL = 2048
BATCH = 8
SEQ = 2048
DEPTH = 1
DEC_BATCH = 32
DEC_SEQ = 4
PAST_LEN = 8192
PAGE_SIZE = 128

HEAD_DIM = 128
FOX_HEADS = 8
FOX_KV_HEADS = 4
MOBA_HEADS = 8
MOBA_KV_HEADS = 4
FOX_GROUP = FOX_HEADS // FOX_KV_HEADS
MOBA_GROUP = MOBA_HEADS // MOBA_KV_HEADS
ROT_DIM = HEAD_DIM // 4
ROPE_THETA = 500000.0
ATTN_SCALE = HEAD_DIM ** -0.5
FOX_Q_BLOCK = 128
MOBA_BLOCK = 256
MOBA_TOPK = 3
MOBA_Q_CHUNK = 32
N_GROUPS = 4
EXPERTS_PER_GROUP = 8
N_EXPERTS = N_GROUPS * EXPERTS_PER_GROUP
TOP_K_IN_GROUP = 2
D_EXPERT = 512
MOE_ROW_BLOCK = 128
RMS_EPS = 1e-6
FOX_Q_W = FOX_HEADS * HEAD_DIM
FOX_KV_W = FOX_KV_HEADS * HEAD_DIM
MOBA_Q_W = MOBA_HEADS * HEAD_DIM
MOBA_KV_W = MOBA_KV_HEADS * HEAD_DIM
SPLIT_SIZES = (FOX_Q_W, FOX_KV_W, FOX_KV_W, FOX_HEADS, MOBA_Q_W, MOBA_KV_W, MOBA_KV_W, D_MODEL, D_MODEL)
SPLIT_OFFSETS = tuple(int(o) for o in np.cumsum(SPLIT_SIZES)[:-1])
D_IN_PROJ = int(sum(SPLIT_SIZES))

kernel_name = "fox_moba_hier_moe_decode_step"


def rmsnorm(x, g):
    xf = x.astype(jnp.float32)
    y = xf * lax.rsqrt(jnp.mean(xf * xf, axis=-1, keepdims=True) + RMS_EPS)
    return (y * g.astype(jnp.float32)).astype(x.dtype)


def rope_partial(x, pos):
    T = x.shape[1]
    half = ROT_DIM // 2
    inv = jnp.power(ROPE_THETA, -jnp.arange(0, ROT_DIM, 2, dtype=jnp.float32) / ROT_DIM)
    ang = pos.astype(jnp.float32)[:, None] * inv[None, :]
    shp = (1, T) + (1,) * (x.ndim - 3) + (half,)
    cos = jnp.cos(ang).reshape(shp)
    sin = jnp.sin(ang).reshape(shp)
    xf = x.astype(jnp.float32)
    x1 = xf[..., :half]
    x2 = xf[..., half:ROT_DIM]
    out = jnp.concatenate([x1 * cos - x2 * sin, x2 * cos + x1 * sin, xf[..., ROT_DIM:]], axis=-1)
    return out.astype(x.dtype)


def gather_pages(cache, layer, page_table):
    g = cache[layer, page_table]
    return g.reshape((g.shape[0], g.shape[1] * g.shape[2]) + g.shape[3:])


def sweep_queries(fn, per_query, q_pos, block):
    T = q_pos.shape[0]
    if T <= block or T % block:
        return fn(*per_query, q_pos)
    n = T // block
    split = lambda a: jnp.moveaxis(a.reshape((a.shape[0], n, block) + a.shape[2:]), 1, 0)
    out = lax.map(lambda a: fn(*a[:-1], a[-1]), tuple(split(a) for a in per_query) + (q_pos.reshape(n, block),))
    out = jnp.moveaxis(out, 0, 1)
    return out.reshape((out.shape[0], T) + out.shape[3:])


def fox_attend(q, cq, q_pos, k, v, ckT, k_pos):
    s = jnp.einsum("bqhgd,bkhd->bhgqk", q, k, preferred_element_type=jnp.float32) * ATTN_SCALE
    bias = jnp.transpose(cq, (0, 2, 3, 1))[..., :, None] - ckT[..., None, :]
    mask = k_pos[None, :] <= q_pos[:, None]
    p = jax.nn.softmax(jnp.where(mask, s + bias, -jnp.inf), axis=-1)
    return jnp.einsum("bhgqk,bkhd->bqhgd", p.astype(v.dtype), v)


def moba_blocks(k, v):
    B, Tk, H, D = k.shape
    n_blk = -(-Tk // MOBA_BLOCK)
    pad = ((0, 0), (0, n_blk * MOBA_BLOCK - Tk), (0, 0), (0, 0))
    kb = jnp.pad(k, pad).reshape(B, n_blk, MOBA_BLOCK, H, D).transpose(0, 3, 1, 2, 4)
    vb = jnp.pad(v, pad).reshape(B, n_blk, MOBA_BLOCK, H, D).transpose(0, 3, 1, 2, 4)
    kmean = jnp.mean(kb.astype(jnp.float32), axis=3)
    return kb, vb, kmean


def moba_attend(q, q_pos, kb, vb, kmean):
    B, HKV, n_blk = kb.shape[:3]
    gate = jnp.einsum("bqhgd,bhnd->bqhgn", q.astype(jnp.float32), kmean)
    q_blk = q_pos // MOBA_BLOCK
    elig = jnp.arange(n_blk)[None, :] < q_blk[:, None]
    gate = jnp.where(elig[None, :, None, None, :], gate, -jnp.inf)
    _, top_idx = lax.top_k(gate, min(MOBA_TOPK, n_blk))
    top_idx = top_idx.astype(jnp.int32)
    qb5 = q_blk.astype(jnp.int32)[None, :, None, None, None]
    own = jnp.broadcast_to(qb5, top_idx.shape[:-1] + (1,))
    sel = jnp.concatenate([top_idx, own], axis=-1)
    sel_valid = jnp.concatenate([top_idx < qb5, jnp.ones(own.shape, bool)], axis=-1)
    bi = jnp.arange(B)[:, None, None, None, None]
    hi = jnp.arange(HKV)[None, None, :, None, None]
    kg = kb[bi, hi, sel]
    vg = vb[bi, hi, sel]
    s = jnp.einsum("bqhgd,bqhgjkd->bqhgjk", q, kg, preferred_element_type=jnp.float32) * ATTN_SCALE
    k_pos = sel[..., None] * MOBA_BLOCK + jnp.arange(MOBA_BLOCK)
    mask = sel_valid[..., None] & (k_pos <= q_pos[None, :, None, None, None, None])
    s = jnp.where(mask, s, -jnp.inf)
    shp = s.shape
    p = jax.nn.softmax(s.reshape(shp[:-2] + (-1,)), axis=-1).reshape(shp)
    return jnp.einsum("bqhgjk,bqhgjkd->bqhgd", p.astype(vg.dtype), vg)


def routed_experts(ht, expert, weight, w_gate, w_up, w_down):
    N = ht.shape[0]
    A = expert.shape[0]
    token = jnp.arange(A, dtype=jnp.int32) // TOP_K_IN_GROUP
    order = jnp.argsort(expert)
    e_sorted = expert[order]
    t_sorted = token[order]
    w_sorted = weight[order]
    counts = jnp.bincount(expert, length=N_EXPERTS).astype(jnp.int32)
    padded = (counts + MOE_ROW_BLOCK - 1) // MOE_ROW_BLOCK * MOE_ROW_BLOCK
    start = jnp.cumsum(counts) - counts
    pend = jnp.cumsum(padded)
    pstart = pend - padded
    dest = pstart[e_sorted] + (jnp.arange(A, dtype=jnp.int32) - start[e_sorted])
    n_blocks = -(-A // MOE_ROW_BLOCK) + N_EXPERTS
    P = n_blocks * MOE_ROW_BLOCK
    buf_tok = jnp.zeros((P,), jnp.int32).at[dest].set(t_sorted)
    buf_w = jnp.zeros((P,), weight.dtype).at[dest].set(w_sorted)
    blk_expert = jnp.searchsorted(pend, jnp.arange(n_blocks, dtype=jnp.int32) * MOE_ROW_BLOCK, side="right")
    blk_expert = jnp.minimum(blk_expert, N_EXPERTS - 1)

    def block_fn(args):
        tok, e = args
        xb = ht[tok]
        hid = jax.nn.silu(xb @ w_gate[e]) * (xb @ w_up[e])
        return hid @ w_down[e]

    y = lax.map(block_fn, (buf_tok.reshape(n_blocks, MOE_ROW_BLOCK), blk_expert))
    y = y.reshape(P, -1) * buf_w[:, None].astype(ht.dtype)
    return jax.ops.segment_sum(y, buf_tok, num_segments=N)


def hier_moe(h, w_router_group, w_router_expert, w_gate, w_up, w_down):
    B, T, D = h.shape
    ht = h.reshape(B * T, D)
    glog = (ht @ w_router_group).astype(jnp.float32)
    grp = jnp.argmax(glog, axis=-1).astype(jnp.int32)
    p_grp = jnp.take_along_axis(jax.nn.softmax(glog, axis=-1), grp[:, None], axis=-1)
    elog = (ht @ w_router_expert).astype(jnp.float32).reshape(B * T, N_GROUPS, EXPERTS_PER_GROUP)
    elog_g = jnp.take_along_axis(elog, grp[:, None, None], axis=1)[:, 0]
    top_val, top_loc = lax.top_k(elog_g, TOP_K_IN_GROUP)
    weight = p_grp * jax.nn.softmax(top_val, axis=-1)
    expert = grp[:, None] * EXPERTS_PER_GROUP + top_loc.astype(jnp.int32)
    out = routed_experts(ht, expert.reshape(-1), weight.reshape(-1), w_gate, w_up, w_down)
    return out.reshape(B, T, D)


def decoder_layer(x, pos, past, lw):
    B, T, _ = x.shape
    h = rmsnorm(x, lw["norm_mix"])
    proj = h @ lw["w_in"]
    qa, ka, va, fa, qb, kb, vb, ga, gb = jnp.split(proj, SPLIT_OFFSETS, axis=-1)
    qa = rmsnorm(qa.reshape(B, T, FOX_KV_HEADS, FOX_GROUP, HEAD_DIM), lw["fox_q_norm"])
    ka = rmsnorm(ka.reshape(B, T, FOX_KV_HEADS, HEAD_DIM), lw["fox_k_norm"])
    va = va.reshape(B, T, FOX_KV_HEADS, HEAD_DIM)
    logf = jax.nn.log_sigmoid((fa + lw["b_forget"]).astype(jnp.float32))
    qb = rope_partial(rmsnorm(qb.reshape(B, T, MOBA_KV_HEADS, MOBA_GROUP, HEAD_DIM), lw["moba_q_norm"]), pos)
    kb = rope_partial(rmsnorm(kb.reshape(B, T, MOBA_KV_HEADS, HEAD_DIM), lw["moba_k_norm"]), pos)
    vb = vb.reshape(B, T, MOBA_KV_HEADS, HEAD_DIM)
    if past is None:
        ka_all, va_all, logf_all, kb_all, vb_all = ka, va, logf, kb, vb
    else:
        pk_a, pv_a, plogf, pk_b, pv_b = past
        ka_all = jnp.concatenate([pk_a.astype(ka.dtype), ka], axis=1)
        va_all = jnp.concatenate([pv_a.astype(va.dtype), va], axis=1)
        logf_all = jnp.concatenate([plogf.astype(jnp.float32), logf], axis=1)
        kb_all = jnp.concatenate([pk_b.astype(kb.dtype), kb], axis=1)
        vb_all = jnp.concatenate([pv_b.astype(vb.dtype), vb], axis=1)
    L = ka_all.shape[1]
    k_pos = jnp.arange(L, dtype=jnp.int32)
    c_all = jnp.cumsum(logf_all, axis=1).reshape(B, L, FOX_KV_HEADS, FOX_GROUP)
    ckT = jnp.transpose(c_all, (0, 2, 3, 1))
    cq = c_all[:, L - T:]
    out_a = sweep_queries(lambda q, c, p: fox_attend(q, c, p, ka_all, va_all, ckT, k_pos), (qa, cq), pos, FOX_Q_BLOCK)
    kblk, vblk, kmean = moba_blocks(kb_all, vb_all)
    out_b = sweep_queries(lambda q, p: moba_attend(q, p, kblk, vblk, kmean), (qb,), pos, MOBA_Q_CHUNK)
    ya = out_a.reshape(B, T, FOX_Q_W) @ lw["w_branch_fox"]
    yb = out_b.reshape(B, T, MOBA_Q_W) @ lw["w_branch_moba"]
    mixed = jax.nn.sigmoid(ga) * ya + jax.nn.sigmoid(gb) * yb
    x = x + mixed @ lw["w_out"]
    x = x + hier_moe(rmsnorm(x, lw["norm_ffn"]), lw["w_router_group"], lw["w_router_expert"],
                     lw["w_exp_gate"], lw["w_exp_up"], lw["w_exp_down"])
    return x, (ka, va, logf, kb, vb)


def setup_inputs(seed: int = 0) -> dict:
    key = jax.random.key(seed)
    ks = jax.random.split(key, 32)
    n_pages = PAST_LEN // PAGE_SIZE
    n_used = DEC_BATCH * n_pages
    n_pool = (n_used * 5) // 4
    nrm = lambda k, shp, s=1.0: jax.random.normal(k, shp, jnp.float32) * s
    gain = lambda k, shp: 1.0 + 0.01 * jax.random.normal(k, shp, jnp.float32)
    page_table = jax.random.permutation(ks[7], n_pool)[:n_used].reshape(DEC_BATCH, n_pages).astype(jnp.int32)
    b_forget = jnp.linspace(1.0, 6.0, FOX_HEADS, dtype=jnp.float32)[None, :] + 0.1 * nrm(ks[10], (DEPTH, FOX_HEADS))
    return {
        "x_prompt": nrm(ks[0], (BATCH, SEQ, D_MODEL)),
        "x_sample": nrm(ks[1], (DEC_BATCH, DEC_SEQ, D_MODEL)),
        "cache_fox_k": nrm(ks[2], (DEPTH, n_pool, PAGE_SIZE, FOX_KV_HEADS, HEAD_DIM)),
        "cache_fox_v": nrm(ks[3], (DEPTH, n_pool, PAGE_SIZE, FOX_KV_HEADS, HEAD_DIM)),
        "cache_fox_logf": jax.nn.log_sigmoid(4.0 + nrm(ks[4], (DEPTH, n_pool, PAGE_SIZE, FOX_HEADS))),
        "cache_moba_k": nrm(ks[5], (DEPTH, n_pool, PAGE_SIZE, MOBA_KV_HEADS, HEAD_DIM)),
        "cache_moba_v": nrm(ks[6], (DEPTH, n_pool, PAGE_SIZE, MOBA_KV_HEADS, HEAD_DIM)),
        "page_table": page_table,
        "norm_mix": gain(ks[8], (DEPTH, D_MODEL)),
        "w_in": nrm(ks[9], (DEPTH, D_MODEL, D_IN_PROJ), D_MODEL ** -0.5),
        "b_forget": b_forget,
        "fox_q_norm": gain(ks[11], (DEPTH, HEAD_DIM)),
        "fox_k_norm": gain(ks[12], (DEPTH, HEAD_DIM)),
        "moba_q_norm": gain(ks[13], (DEPTH, HEAD_DIM)),
        "moba_k_norm": gain(ks[14], (DEPTH, HEAD_DIM)),
        "w_branch_fox": nrm(ks[15], (DEPTH, FOX_Q_W, D_MODEL), FOX_Q_W ** -0.5),
        "w_branch_moba": nrm(ks[16], (DEPTH, MOBA_Q_W, D_MODEL), MOBA_Q_W ** -0.5),
        "w_out": nrm(ks[17], (DEPTH, D_MODEL, D_MODEL), D_MODEL ** -0.5),
        "norm_ffn": gain(ks[18], (DEPTH, D_MODEL)),
        "w_router_group": nrm(ks[19], (DEPTH, D_MODEL, N_GROUPS), D_MODEL ** -0.5),
        "w_router_expert": nrm(ks[20], (DEPTH, D_MODEL, N_EXPERTS), D_MODEL ** -0.5),
        "w_exp_gate": nrm(ks[21], (DEPTH, N_EXPERTS, D_MODEL, D_EXPERT), D_MODEL ** -0.5),
        "w_exp_up": nrm(ks[22], (DEPTH, N_EXPERTS, D_MODEL, D_EXPERT), D_MODEL ** -0.5),
        "w_exp_down": nrm(ks[23], (DEPTH, N_EXPERTS, D_EXPERT, D_MODEL), D_EXPERT ** -0.5),
    }


def reference(x_prompt, x_sample, cache_fox_k, cache_fox_v, cache_fox_logf, cache_moba_k, cache_moba_v,
              page_table, norm_mix, w_in, b_forget, fox_q_norm, fox_k_norm, moba_q_norm, moba_k_norm,
              w_branch_fox, w_branch_moba, w_out, norm_ffn, w_router_group, w_router_expert,
              w_exp_gate, w_exp_up, w_exp_down):
    past_len = page_table.shape[1] * PAGE_SIZE
    pos_p = jnp.arange(x_prompt.shape[1], dtype=jnp.int32)
    pos_s = past_len + jnp.arange(x_sample.shape[1], dtype=jnp.int32)
    xp, xs = x_prompt, x_sample
    rows_p, rows_s = [], []
    for l in range(DEPTH):
        lw = dict(norm_mix=norm_mix[l], w_in=w_in[l], b_forget=b_forget[l],
                  fox_q_norm=fox_q_norm[l], fox_k_norm=fox_k_norm[l],
                  moba_q_norm=moba_q_norm[l], moba_k_norm=moba_k_norm[l],
                  w_branch_fox=w_branch_fox[l], w_branch_moba=w_branch_moba[l], w_out=w_out[l],
                  norm_ffn=norm_ffn[l], w_router_group=w_router_group[l],
                  w_router_expert=w_router_expert[l], w_exp_gate=w_exp_gate[l],
                  w_exp_up=w_exp_up[l], w_exp_down=w_exp_down[l])
        xp, rp = decoder_layer(xp, pos_p, None, lw)
        past = (gather_pages(cache_fox_k, l, page_table), gather_pages(cache_fox_v, l, page_table),
                gather_pages(cache_fox_logf, l, page_table), gather_pages(cache_moba_k, l, page_table),
                gather_pages(cache_moba_v, l, page_table))
        xs, rs = decoder_layer(xs, pos_s, past, lw)
        rows_p.append(rp)
        rows_s.append(rs)
    fox_k_prompt = jnp.stack([r[0] for r in rows_p])
    fox_v_prompt = jnp.stack([r[1] for r in rows_p])
    fox_logf_prompt = jnp.stack([r[2] for r in rows_p])
    moba_k_prompt = jnp.stack([r[3] for r in rows_p])
    moba_v_prompt = jnp.stack([r[4] for r in rows_p])
    fox_k_sample = jnp.stack([r[0] for r in rows_s])
    fox_v_sample = jnp.stack([r[1] for r in rows_s])
    fox_logf_sample = jnp.stack([r[2] for r in rows_s])
    moba_k_sample = jnp.stack([r[3] for r in rows_s])
    moba_v_sample = jnp.stack([r[4] for r in rows_s])
    return (xp, xs, fox_k_prompt, fox_v_prompt, fox_logf_prompt, moba_k_prompt, moba_v_prompt,
            fox_k_sample, fox_v_sample, fox_logf_sample, moba_k_sample, moba_v_sample)
```

```python
import functools

import numpy as np
import jax
import jax.numpy as jnp
from jax import lax
from jax.experimental import pallas as pl
from jax.experimental.pallas import tpu as pltpu

F32 = jnp.float32
BF16 = jnp.bfloat16
I32 = jnp.int32
HIGHEST = lax.Precision.HIGHEST

LANES = 128
D_MODEL = 2048
HEAD_DIM = 128
N_HEADS = 8
N_KV = 4
GROUP = N_HEADS // N_KV
ROT_DIM = HEAD_DIM // 4
ROPE_THETA = 500000.0
ATTN_SCALE = HEAD_DIM ** -0.5
MOBA_BLOCK = 256
MOBA_TOPK = 3
PAGE = 128
N_GROUPS = 4
EXPERTS_PER_GROUP = 8
N_EXPERTS = N_GROUPS * EXPERTS_PER_GROUP
D_EXPERT = 512
RMS_EPS = 1e-6
NEG = -1e30
Q_W = N_HEADS * HEAD_DIM
KV_W = N_KV * HEAD_DIM
PROJ_W = 2 * (Q_W + 2 * KV_W) + 2 * D_MODEL
PROJ_TILE = 512
COL_FOX_Q, COL_FOX_K, COL_FOX_V = 0, Q_W, Q_W + KV_W
COL_MOBA_Q = Q_W + 2 * KV_W
COL_MOBA_K, COL_MOBA_V = COL_MOBA_Q + Q_W, COL_MOBA_Q + Q_W + KV_W
COL_GA = 2 * (Q_W + 2 * KV_W)
COL_GB = COL_GA + D_MODEL
VMEM_LIMIT = 56 * 1024 * 1024


def _cparams(sem):
    return pltpu.CompilerParams(dimension_semantics=sem, vmem_limit_bytes=VMEM_LIMIT)


def _sds(shape, dtype):
    return jax.ShapeDtypeStruct(shape, dtype)


def _head_rmsnorm(y, gain):
    ms = jnp.mean(y * y, axis=-1, keepdims=True)
    return y * lax.rsqrt(ms + RMS_EPS) * gain


def _inproj_body(x_ref, g_ref, w_ref, wf_ref, bf_ref, cg_ref, cos_ref, sa_ref, sb_ref,
                 proj_ref, logf_ref, h_scr):
    j = pl.program_id(1)

    @pl.when(j == 0)
    def _():
        x = x_ref[...]
        ms = jnp.mean(x * x, axis=-1, keepdims=True)
        h = (x * lax.rsqrt(ms + RMS_EPS) * g_ref[...]).astype(BF16)
        h_scr[...] = h
        z = jnp.dot(h, wf_ref[...], preferred_element_type=F32) + bf_ref[...]
        logf_ref[...] = -(jnp.maximum(-z, 0.0) + jnp.log1p(jnp.exp(-jnp.abs(z))))

    acc = jnp.dot(h_scr[...], w_ref[...], preferred_element_type=F32)
    tq0, tk0, tv0 = COL_FOX_Q // PROJ_TILE, COL_FOX_K // PROJ_TILE, COL_FOX_V // PROJ_TILE
    mq0, mk0, mv0 = COL_MOBA_Q // PROJ_TILE, COL_MOBA_K // PROJ_TILE, COL_MOBA_V // PROJ_TILE
    fox_normed = (j >= tq0) & (j < tv0)
    moba_normed = (j >= mq0) & (j < mv0)
    del tk0, mk0

    @pl.when(jnp.logical_not(fox_normed | moba_normed))
    def _():
        proj_ref[...] = acc

    @pl.when(fox_normed)
    def _():
        for hh in range(PROJ_TILE // HEAD_DIM):
            sl = slice(hh * HEAD_DIM, (hh + 1) * HEAD_DIM)
            proj_ref[:, sl] = _head_rmsnorm(acc[:, sl], cg_ref[:, sl])

    @pl.when(moba_normed)
    def _():
        cos, sa, sb = cos_ref[...], sa_ref[...], sb_ref[...]
        for hh in range(PROJ_TILE // HEAD_DIM):
            sl = slice(hh * HEAD_DIM, (hh + 1) * HEAD_DIM)
            y = _head_rmsnorm(acc[:, sl], cg_ref[:, sl])
            half = ROT_DIM // 2
            y = (y * cos + pltpu.roll(y, half, axis=1) * sa
                 + pltpu.roll(y, HEAD_DIM - half, axis=1) * sb)
            proj_ref[:, sl] = y


def _inproj(x2, g, w_main, w_f, b_f, colgain, tabs, *, tm):
    n = x2.shape[0]
    ntab = tabs[0].shape[0] // tm
    tab_spec = pl.BlockSpec((tm, LANES), lambda i, j: (i % ntab, 0))
    return pl.pallas_call(
        _inproj_body,
        grid=(n // tm, PROJ_W // PROJ_TILE),
        in_specs=[
            pl.BlockSpec((tm, D_MODEL), lambda i, j: (i, 0)),
            pl.BlockSpec((1, D_MODEL), lambda i, j: (0, 0)),
            pl.BlockSpec((D_MODEL, PROJ_TILE), lambda i, j: (0, j)),
            pl.BlockSpec((D_MODEL, LANES), lambda i, j: (0, 0)),
            pl.BlockSpec((1, LANES), lambda i, j: (0, 0)),
            pl.BlockSpec((1, PROJ_TILE), lambda i, j: (0, j)),
            tab_spec, tab_spec, tab_spec,
        ],
        out_specs=[pl.BlockSpec((tm, PROJ_TILE), lambda i, j: (i, j)),
                   pl.BlockSpec((tm, LANES), lambda i, j: (i, 0))],
        out_shape=[_sds((n, PROJ_W), F32), _sds((n, LANES), F32)],
        scratch_shapes=[pltpu.VMEM((tm, D_MODEL), BF16)],
        compiler_params=_cparams(("parallel", "arbitrary")),
        name="inproj",
    )(x2, g, w_main, w_f, b_f, colgain, *tabs)


def _lower_tri(n, strict=False):
    r = lax.broadcasted_iota(I32, (n, n), 0)
    c = lax.broadcasted_iota(I32, (n, n), 1)
    return (c < r) if strict else (c <= r)


def _cumsum_body(x_ref, o_ref, carry):
    @pl.when(pl.program_id(1) == 0)
    def _():
        carry[...] = jnp.zeros_like(carry)

    x = x_ref[0]
    tc = x.shape[0]
    tri = _lower_tri(tc).astype(F32)
    cs = jnp.dot(tri, x, precision=HIGHEST, preferred_element_type=F32) + carry[...]
    o_ref[0] = cs
    carry[...] = cs[tc - 1:tc, :]


def _cumsum(logf3, *, tc):
    b, t, w = logf3.shape
    return pl.pallas_call(
        _cumsum_body,
        grid=(b, t // tc),
        in_specs=[pl.BlockSpec((1, tc, w), lambda i, j: (i, j, 0))],
        out_specs=pl.BlockSpec((1, tc, w), lambda i, j: (i, j, 0)),
        out_shape=_sds((b, t, w), F32),
        scratch_shapes=[pltpu.VMEM((1, w), F32)],
        compiler_params=_cparams(("parallel", "arbitrary")),
        name="logf_cumsum",
    )(logf3)


def _paged_cumsum_body(pt_ref, *refs, pages):
    del pt_ref
    page_refs = refs[:pages]
    new_ref = refs[pages]
    o_ref, onew_ref, carry = refs[pages + 1:]
    s = pl.program_id(1)

    @pl.when(s == 0)
    def _():
        carry[...] = jnp.zeros_like(carry)

    tri = _lower_tri(PAGE).astype(F32)
    c = carry[...]
    for p in range(pages):
        cs = jnp.dot(tri, page_refs[p][0], precision=HIGHEST, preferred_element_type=F32) + c
        o_ref[0, p * PAGE:(p + 1) * PAGE, :] = cs
        c = cs[PAGE - 1:PAGE, :]
    carry[...] = c

    @pl.when(s == pl.num_programs(1) - 1)
    def _():
        new = new_ref[0][:, 0:N_HEADS]
        acc = c
        rows = []
        for r in range(new.shape[0]):
            acc = acc + new[r:r + 1, :]
            rows.append(acc)
        onew_ref[0] = jnp.concatenate(rows, axis=0)


def _paged_cumsum(pt_flat, cache_logf3, logf_new, *, n_pages, pages):
    b = logf_new.shape[0]
    rows_new = logf_new.shape[1]

    def page_map(p):
        return lambda i, s, pt: (pt[i * n_pages + s * pages + p], 0, 0)

    grid_spec = pltpu.PrefetchScalarGridSpec(
        num_scalar_prefetch=1,
        grid=(b, n_pages // pages),
        in_specs=[pl.BlockSpec((1, PAGE, N_HEADS), page_map(p)) for p in range(pages)]
        + [pl.BlockSpec((1, rows_new, LANES), lambda i, s, pt: (i, 0, 0))],
        out_specs=[pl.BlockSpec((1, pages * PAGE, N_HEADS), lambda i, s, pt: (i, s, 0)),
                   pl.BlockSpec((1, rows_new, N_HEADS), lambda i, s, pt: (i, 0, 0))],
        scratch_shapes=[pltpu.VMEM((1, N_HEADS), F32)],
    )
    return pl.pallas_call(
        functools.partial(_paged_cumsum_body, pages=pages),
        grid_spec=grid_spec,
        out_shape=[_sds((b, n_pages * PAGE, N_HEADS), F32), _sds((b, rows_new, N_HEADS), F32)],
        compiler_params=_cparams(("parallel", "arbitrary")),
        name="paged_logf_cumsum",
    )(pt_flat, *([cache_logf3] * pages), logf_new)


def _softmax_tile(t, rowshift, v, carry):
    m, l, acc = carry
    m_new = jnp.maximum(m, jnp.max(t, axis=-1, keepdims=True) + rowshift)
    p = jnp.exp(t + (rowshift - m_new))
    alpha = jnp.exp(m - m_new)
    l = alpha * l + jnp.sum(p, axis=-1, keepdims=True)
    acc = alpha * acc + jnp.dot(p.astype(BF16), v, preferred_element_type=F32)
    return m_new, l, acc


def _qk(q, k):
    return lax.dot_general(q, k, (((1,), (1,)), ((), ())), preferred_element_type=F32)


def _fox_body(q_ref, k_ref, v_ref, cq_ref, ck_ref, o_ref, *, tq):
    h = pl.program_id(1)
    qi = pl.program_id(2)
    q = (q_ref[0] * ATTN_SCALE).astype(BF16)
    lane = lax.broadcasted_iota(I32, (tq, LANES), 1)
    cq = jnp.sum(jnp.where(lane == h, cq_ref[0], 0.0), axis=-1, keepdims=True)

    def tile(start, diag, carry):
        k = k_ref[0, pl.ds(start, tq), :].astype(BF16)
        v = v_ref[0, pl.ds(start, tq), :].astype(BF16)
        ck = ck_ref[0, pl.ds(h, 1), pl.ds(start, tq)]
        t = _qk(q, k) - ck
        if diag:
            t = jnp.where(_lower_tri(tq), t, NEG)
        return _softmax_tile(t, cq, v, carry)

    init = (jnp.full((tq, 1), NEG, F32), jnp.zeros((tq, 1), F32), jnp.zeros((tq, HEAD_DIM), F32))
    carry = tile(pl.multiple_of(qi * tq, tq), True, init)
    carry = lax.fori_loop(0, qi, lambda j, c: tile(pl.multiple_of(j * tq, tq), False, c), carry)
    _, l, acc = carry
    o_ref[0] = (acc / l).astype(o_ref.dtype)


def _fox_prompt(proj3, c3, ck_t, *, tq):
    b, t, _ = proj3.shape
    qb, kb, vb = COL_FOX_Q // HEAD_DIM, COL_FOX_K // HEAD_DIM, COL_FOX_V // HEAD_DIM
    return pl.pallas_call(
        functools.partial(_fox_body, tq=tq),
        grid=(b, N_HEADS, t // tq),
        in_specs=[
            pl.BlockSpec((1, tq, HEAD_DIM), lambda i, h, q: (i, q, qb + h)),
            pl.BlockSpec((1, t, HEAD_DIM), lambda i, h, q: (i, 0, kb + h // GROUP)),
            pl.BlockSpec((1, t, HEAD_DIM), lambda i, h, q: (i, 0, vb + h // GROUP)),
            pl.BlockSpec((1, tq, LANES), lambda i, h, q: (i, q, 0)),
            pl.BlockSpec((1, N_HEADS, t), lambda i, h, q: (i, 0, 0)),
        ],
        out_specs=pl.BlockSpec((1, tq, HEAD_DIM), lambda i, h, q: (i, q, h)),
        out_shape=_sds((b, t, Q_W), BF16),
        compiler_params=_cparams(("parallel", "parallel", "arbitrary")),
        name="fox_prompt",
    )(proj3, proj3, proj3, c3, ck_t)


def _topk_select(g, elig, n_cand):
    lane = lax.broadcasted_iota(I32, g.shape, 1)
    g = jnp.where(elig, g, -jnp.inf)
    cnt = jnp.zeros(g.shape, F32)
    for m in range(n_cand):
        gm = g[:, m:m + 1]
        beats = (gm > g) | ((gm == g) & (m < lane))
        cnt = cnt + jnp.where(beats, 1.0, 0.0)
    return elig & (cnt < MOBA_TOPK)


def _moba_body(q_ref, k_ref, v_ref, o_ref, kmean_scr, *, n_blk):
    qi = pl.program_id(2)
    tq = MOBA_BLOCK

    @pl.when(qi == 0)
    def _():
        kmean_scr[...] = jnp.zeros_like(kmean_scr)
        for n in range(n_blk):
            kb = k_ref[0, n * tq:(n + 1) * tq, :]
            kmean_scr[n:n + 1, :] = jnp.mean(kb, axis=0, keepdims=True)

    qf = q_ref[0]
    q = (qf * ATTN_SCALE).astype(BF16)
    gate = lax.dot_general(qf, kmean_scr[...], (((1,), (1,)), ((), ())),
                           precision=HIGHEST, preferred_element_type=F32)
    lane = lax.broadcasted_iota(I32, (tq, LANES), 1)
    sel = jnp.where(_topk_select(gate, lane < qi, n_blk), 1.0, 0.0)

    def tile(start, rowbias, diag, carry):
        k = k_ref[0, pl.ds(start, tq), :].astype(BF16)
        v = v_ref[0, pl.ds(start, tq), :].astype(BF16)
        t = _qk(q, k)
        if diag:
            t = jnp.where(_lower_tri(tq), t, NEG)
        else:
            t = t + rowbias
        return _softmax_tile(t, 0.0, v, carry)

    def past(n, carry):
        flag = jnp.sum(jnp.where(lane == n, sel, 0.0), axis=-1, keepdims=True)
        return tile(pl.multiple_of(n * tq, tq), (flag - 1.0) * (-NEG), False, carry)

    init = (jnp.full((tq, 1), NEG, F32), jnp.zeros((tq, 1), F32), jnp.zeros((tq, HEAD_DIM), F32))
    carry = tile(pl.multiple_of(qi * tq, tq), None, True, init)
    carry = lax.fori_loop(0, qi, past, carry)
    _, l, acc = carry
    o_ref[0] = (acc / l).astype(o_ref.dtype)


def _moba_prompt(proj3):
    b, t, _ = proj3.shape
    assert t % MOBA_BLOCK == 0 and t // MOBA_BLOCK <= LANES
    qb, kb, vb = COL_MOBA_Q // HEAD_DIM, COL_MOBA_K // HEAD_DIM, COL_MOBA_V // HEAD_DIM
    tq = MOBA_BLOCK
    return pl.pallas_call(
        functools.partial(_moba_body, n_blk=t // MOBA_BLOCK),
        grid=(b, N_HEADS, t // tq),
        in_specs=[
            pl.BlockSpec((1, tq, HEAD_DIM), lambda i, h, q: (i, q, qb + h)),
            pl.BlockSpec((1, t, HEAD_DIM), lambda i, h, q: (i, 0, kb + h // GROUP)),
            pl.BlockSpec((1, t, HEAD_DIM), lambda i, h, q: (i, 0, vb + h // GROUP)),
        ],
        out_specs=pl.BlockSpec((1, tq, HEAD_DIM), lambda i, h, q: (i, q, h)),
        out_shape=_sds((b, t, Q_W), BF16),
        scratch_shapes=[pltpu.VMEM((LANES, HEAD_DIM), F32)],
        compiler_params=_cparams(("parallel", "parallel", "arbitrary")),
        name="moba_prompt",
    )(proj3, proj3, proj3)


def _fox_decode_body(pt_ref, *refs, pages, rows):
    del pt_ref
    q_ref, cq_ref, nck_ref, mask_ref, knew_ref, vnew_ref, bnew_ref = refs[:7]
    k_refs = refs[7:7 + pages]
    v_refs = refs[7 + pages:7 + 2 * pages]
    o_ref, m_scr, l_scr, acc_scr = refs[7 + 2 * pages:]
    s = pl.program_id(1)
    cols = PAGE * N_KV

    @pl.when(s == 0)
    def _():
        m_scr[...] = jnp.full_like(m_scr, NEG)
        l_scr[...] = jnp.zeros_like(l_scr)
        acc_scr[...] = jnp.zeros_like(acc_scr)

    q = (q_ref[0] * ATTN_SCALE).astype(BF16)
    cq = cq_ref[0]
    carry = (m_scr[...], l_scr[...], acc_scr[...])
    for p in range(pages):
        k = k_refs[p][0].astype(BF16)
        v = v_refs[p][0].astype(BF16)
        nck = nck_ref[0, :, p * cols:(p + 1) * cols]
        t = _qk(q, k) + jnp.concatenate([nck] * (rows // N_HEADS), axis=0) + mask_ref[...]
        carry = _softmax_tile(t, cq, v, carry)
    m_scr[...], l_scr[...], acc_scr[...] = carry

    @pl.when(s == pl.num_programs(1) - 1)
    def _():
        t = _qk(q, knew_ref[0].astype(BF16)) + bnew_ref[0]
        _, l, acc = _softmax_tile(t, cq, vnew_ref[0].astype(BF16), carry)
        o_ref[0] = (acc / l).astype(o_ref.dtype)


def _fox_decode(pt_flat, q32, cq32, nck_rep, mask, k_new, v_new, bias_new, cache_k, cache_v,
                *, n_pages, pages):
    b, rows, _ = q32.shape
    cols = PAGE * N_KV
    n_new = k_new.shape[1]

    def page_map(p):
        return lambda i, s, pt: (pt[i * n_pages + s * pages + p], 0, 0)

    per_b = lambda i, s, pt: (i, 0, 0)
    grid_spec = pltpu.PrefetchScalarGridSpec(
        num_scalar_prefetch=1,
        grid=(b, n_pages // pages),
        in_specs=[
            pl.BlockSpec((1, rows, HEAD_DIM), per_b),
            pl.BlockSpec((1, rows, 1), per_b),
            pl.BlockSpec((1, N_HEADS, pages * cols), lambda i, s, pt: (i, 0, s)),
            pl.BlockSpec((rows, cols), lambda i, s, pt: (0, 0)),
            pl.BlockSpec((1, n_new, HEAD_DIM), per_b),
            pl.BlockSpec((1, n_new, HEAD_DIM), per_b),
            pl.BlockSpec((1, rows, n_new), per_b),
        ] + [pl.BlockSpec((1, cols, HEAD_DIM), page_map(p)) for p in range(pages)] * 2,
        out_specs=pl.BlockSpec((1, rows, HEAD_DIM), per_b),
        scratch_shapes=[pltpu.VMEM((rows, 1), F32), pltpu.VMEM((rows, 1), F32),
                        pltpu.VMEM((rows, HEAD_DIM), F32)],
    )
    return pl.pallas_call(
        functools.partial(_fox_decode_body, pages=pages, rows=rows),
        grid_spec=grid_spec,
        out_shape=_sds((b, rows, HEAD_DIM), BF16),
        compiler_params=_cparams(("parallel", "arbitrary")),
        name="fox_decode",
    )(pt_flat, q32, cq32, nck_rep, mask, k_new, v_new, bias_new,
      *([cache_k] * pages), *([cache_v] * pages))


def _moba_decode_body(pt_ref, *refs, nb, n_past, rows):
    del pt_ref
    ppb = MOBA_BLOCK // PAGE
    pages = nb * ppb
    q_ref, mask_ref, knew_ref, vnew_ref, bnew_ref = refs[:5]
    k_refs = refs[5:5 + pages]
    v_refs = refs[5 + pages:5 + 2 * pages]
    o_ref, m_scr, l_scr, g_scr, acc_scr = refs[5 + 2 * pages:]
    s = pl.program_id(1)

    @pl.when(s == 0)
    def _():
        m_scr[...] = jnp.zeros_like(m_scr)
        l_scr[...] = jnp.zeros_like(l_scr)
        g_scr[...] = jnp.zeros_like(g_scr)

    qf = q_ref[0]
    q = (qf * ATTN_SCALE).astype(BF16)
    lane = lax.broadcasted_iota(I32, (rows, LANES), 1)
    for j in range(nb):
        n = s * nb + j
        kf = jnp.concatenate([k_refs[j * ppb + p][0] for p in range(ppb)], axis=0)
        vf = jnp.concatenate([v_refs[j * ppb + p][0] for p in range(ppb)], axis=0)
        t = _qk(q, kf.astype(BF16)) + mask_ref[...]
        m_n = jnp.max(t, axis=-1, keepdims=True)
        p_ = jnp.exp(t - m_n)
        l_n = jnp.sum(p_, axis=-1, keepdims=True)
        acc_scr[n] = jnp.dot(p_.astype(BF16), vf.astype(BF16), preferred_element_type=F32)
        s8 = jnp.sum(kf.reshape(MOBA_BLOCK * N_KV // 8, 8, HEAD_DIM), axis=0)
        ks = s8[0:N_KV] + s8[N_KV:2 * N_KV]
        ks_h = jnp.concatenate([ks[h // GROUP:h // GROUP + 1] for h in range(N_HEADS)], axis=0)
        ks_r = jnp.concatenate([ks_h] * (rows // N_HEADS), axis=0)
        g_n = jnp.sum(qf * ks_r, axis=-1, keepdims=True) * (1.0 / MOBA_BLOCK)
        m_scr[...] = jnp.where(lane == n, m_n, m_scr[...])
        l_scr[...] = jnp.where(lane == n, l_n, l_scr[...])
        g_scr[...] = jnp.where(lane == n, g_n, g_scr[...])

    @pl.when(s == pl.num_programs(1) - 1)
    def _():
        t = _qk(q, knew_ref[0].astype(BF16)) + bnew_ref[...]
        m_own = jnp.max(t, axis=-1, keepdims=True)
        p_own = jnp.exp(t - m_own)
        l_own = jnp.sum(p_own, axis=-1, keepdims=True)
        acc_own = jnp.dot(p_own.astype(BF16), vnew_ref[0].astype(BF16), preferred_element_type=F32)
        sel = _topk_select(g_scr[...], lane < n_past, n_past)
        m_all = m_scr[...]
        m_x = jnp.maximum(m_own, jnp.max(jnp.where(sel, m_all, NEG), axis=-1, keepdims=True))
        w = jnp.where(sel, jnp.exp(m_all - m_x), 0.0)
        w_own = jnp.exp(m_own - m_x)
        den = w_own * l_own + jnp.sum(w * l_scr[...], axis=-1, keepdims=True)
        num = w_own * acc_own
        for n in range(n_past):
            num = num + w[:, n:n + 1] * acc_scr[n]
        o_ref[0] = (num / den).astype(o_ref.dtype)


def _moba_decode(pt_flat, q32, mask, k_new, v_new, bias_new, cache_k, cache_v, *, n_pages, nb):
    b, rows, _ = q32.shape
    ppb = MOBA_BLOCK // PAGE
    pages = nb * ppb
    n_past = n_pages // ppb
    assert n_past <= LANES and n_past % nb == 0
    cols = PAGE * N_KV
    n_new = k_new.shape[1]

    def page_map(p):
        return lambda i, s, pt: (pt[i * n_pages + s * pages + p], 0, 0)

    per_b = lambda i, s, pt: (i, 0, 0)
    grid_spec = pltpu.PrefetchScalarGridSpec(
        num_scalar_prefetch=1,
        grid=(b, n_past // nb),
        in_specs=[
            pl.BlockSpec((1, rows, HEAD_DIM), per_b),
            pl.BlockSpec((rows, ppb * cols), lambda i, s, pt: (0, 0)),
            pl.BlockSpec((1, n_new, HEAD_DIM), per_b),
            pl.BlockSpec((1, n_new, HEAD_DIM), per_b),
            pl.BlockSpec((rows, n_new), lambda i, s, pt: (0, 0)),
        ] + [pl.BlockSpec((1, cols, HEAD_DIM), page_map(p)) for p in range(pages)] * 2,
        out_specs=pl.BlockSpec((1, rows, HEAD_DIM), per_b),
        scratch_shapes=[pltpu.VMEM((rows, LANES), F32), pltpu.VMEM((rows, LANES), F32),
                        pltpu.VMEM((rows, LANES), F32), pltpu.VMEM((n_past, rows, HEAD_DIM), F32)],
    )
    return pl.pallas_call(
        functools.partial(_moba_decode_body, nb=nb, n_past=n_past, rows=rows),
        grid_spec=grid_spec,
        out_shape=_sds((b, rows, HEAD_DIM), BF16),
        compiler_params=_cparams(("parallel", "arbitrary")),
        name="moba_decode",
    )(pt_flat, q32, mask, k_new, v_new, bias_new, *([cache_k] * pages), *([cache_v] * pages))


def _outproj_body(oa_ref, ob_ref, ga_ref, gb_ref, x_ref, wa_ref, wb_ref, wo_ref, o_ref):
    ya = jnp.dot(oa_ref[...], wa_ref[...], preferred_element_type=F32)
    yb = jnp.dot(ob_ref[...], wb_ref[...], preferred_element_type=F32)
    mixed = jax.nn.sigmoid(ga_ref[...]) * ya + jax.nn.sigmoid(gb_ref[...]) * yb
    o_ref[...] = x_ref[...] + jnp.dot(mixed.astype(BF16), wo_ref[...], preferred_element_type=F32)


def _outproj(oa, ob, proj, x2, wa, wb, wo, *, tm):
    n = x2.shape[0]
    const = lambda i: (0, 0)
    once = pl.Buffered(1)
    return pl.pallas_call(
        _outproj_body,
        grid=(n // tm,),
        in_specs=[
            pl.BlockSpec((tm, Q_W), lambda i: (i, 0)),
            pl.BlockSpec((tm, Q_W), lambda i: (i, 0)),
            pl.BlockSpec((tm, D_MODEL), lambda i: (i, COL_GA // D_MODEL)),
            pl.BlockSpec((tm, D_MODEL), lambda i: (i, COL_GB // D_MODEL)),
            pl.BlockSpec((tm, D_MODEL), lambda i: (i, 0)),
            pl.BlockSpec((Q_W, D_MODEL), const, pipeline_mode=once),
            pl.BlockSpec((Q_W, D_MODEL), const, pipeline_mode=once),
            pl.BlockSpec((D_MODEL, D_MODEL), const, pipeline_mode=once),
        ],
        out_specs=pl.BlockSpec((tm, D_MODEL), lambda i: (i, 0)),
        out_shape=_sds((n, D_MODEL), F32),
        compiler_params=_cparams(("parallel",)),
        name="outproj",
    )(oa, ob, proj, proj, x2, wa, wb, wo)


def _first_lane(hit, lane):
    return jnp.min(jnp.where(hit, lane.astype(F32), float(LANES)), axis=-1, keepdims=True).astype(I32)


def _router_body(x_ref, g_ref, wr_ref, hn_ref, ii_ref, iw_ref, cnt_ref, cnt_scr):
    @pl.when(pl.program_id(0) == 0)
    def _():
        cnt_scr[...] = jnp.zeros_like(cnt_scr)

    x = x_ref[...]
    tm = x.shape[0]
    ms = jnp.mean(x * x, axis=-1, keepdims=True)
    hn = x * lax.rsqrt(ms + RMS_EPS) * g_ref[...]
    hn_ref[...] = hn
    logits = jnp.dot(hn, wr_ref[...], precision=HIGHEST, preferred_element_type=F32)
    lane = lax.broadcasted_iota(I32, (tm, LANES), 1)
    is_g = lane < N_GROUPS
    gl = jnp.where(is_g, logits, -jnp.inf)
    gmax = jnp.max(gl, axis=-1, keepdims=True)
    grp = _first_lane(gl == gmax, lane)
    p_grp = 1.0 / jnp.sum(jnp.where(is_g, jnp.exp(gl - gmax), 0.0), axis=-1, keepdims=True)
    e_lane = lane - N_GROUPS
    shift = EXPERTS_PER_GROUP.bit_length() - 1
    in_grp = (e_lane < N_EXPERTS) & (lax.shift_right_arithmetic(e_lane, shift) == grp)
    el = jnp.where(in_grp, logits, -jnp.inf)
    v1 = jnp.max(el, axis=-1, keepdims=True)
    i1 = _first_lane(el == v1, lane)
    el2 = jnp.where(lane == i1, -jnp.inf, el)
    v2 = jnp.max(el2, axis=-1, keepdims=True)
    i2 = _first_lane(el2 == v2, lane)
    r = jnp.exp(v2 - v1)
    w1 = p_grp / (1.0 + r)
    w2 = p_grp * r / (1.0 + r)
    e1 = i1 - N_GROUPS
    e2 = i2 - N_GROUPS
    oh = jnp.where((lane == e1) | (lane == e2), 1.0, 0.0)
    before = cnt_scr[...] + jnp.dot(_lower_tri(tm, strict=True).astype(BF16), oh.astype(BF16),
                                    preferred_element_type=F32)
    r1 = jnp.sum(jnp.where(lane == e1, before, 0.0), axis=-1, keepdims=True)
    r2 = jnp.sum(jnp.where(lane == e2, before, 0.0), axis=-1, keepdims=True)
    cnt_scr[...] = cnt_scr[...] + jnp.sum(oh, axis=0, keepdims=True)
    cnt_ref[...] = cnt_scr[...].astype(I32)
    ii = jnp.where(lane == 0, e1, jnp.where(lane == 1, e2, jnp.where(
        lane == 2, r1.astype(I32), jnp.where(lane == 3, r2.astype(I32), 0))))
    ii_ref[...] = ii
    iw_ref[...] = jnp.where(lane == 0, w1, jnp.where(lane == 1, w2, 0.0))


def _router(x1, g, wr, *, tm):
    n = x1.shape[0]
    return pl.pallas_call(
        _router_body,
        grid=(n // tm,),
        in_specs=[pl.BlockSpec((tm, D_MODEL), lambda i: (i, 0)),
                  pl.BlockSpec((1, D_MODEL), lambda i: (0, 0)),
                  pl.BlockSpec((D_MODEL, LANES), lambda i: (0, 0))],
        out_specs=[pl.BlockSpec((tm, D_MODEL), lambda i: (i, 0)),
                   pl.BlockSpec((tm, LANES), lambda i: (i, 0)),
                   pl.BlockSpec((tm, LANES), lambda i: (i, 0)),
                   pl.BlockSpec((1, LANES), lambda i: (0, 0))],
        out_shape=[_sds((n, D_MODEL), F32), _sds((n, LANES), I32), _sds((n, LANES), F32),
                   _sds((1, LANES), I32)],
        scratch_shapes=[pltpu.VMEM((1, LANES), F32)],
        compiler_params=_cparams(("arbitrary",)),
        name="moe_router",
    )(x1, g, wr)


def _row_token_body(dest_ref, tok_ref, *, n_tok, n_rows):
    def zero(i, c):
        tok_ref[i] = 0
        return c

    lax.fori_loop(0, n_rows, zero, 0, unroll=8)

    def put(t, c):
        tok_ref[dest_ref[t]] = t
        tok_ref[dest_ref[n_tok + t]] = t
        return c

    lax.fori_loop(0, n_tok, put, 0, unroll=8)


def _row_token(dest_flat, *, n_rows):
    n_tok = dest_flat.shape[0] // 2
    return pl.pallas_call(
        functools.partial(_row_token_body, n_tok=n_tok, n_rows=n_rows),
        in_specs=[pl.BlockSpec(memory_space=pltpu.SMEM)],
        out_specs=pl.BlockSpec(memory_space=pltpu.SMEM),
        out_shape=_sds((n_rows,), I32),
        name="moe_row_token",
    )(dest_flat)


def _row_copy(src_hbm, src_row, buf, buf_row, sem):
    return pltpu.make_async_copy(src_hbm.at[pl.ds(src_row, 1)], buf.at[pl.ds(buf_row, 1)], sem)


def _row_gather_start(src_hbm, idx_ref, base, buf, row0, sem, n):
    def body(r, c):
        _row_copy(src_hbm, idx_ref[base + r], buf, row0 + r, sem).start()
        return c

    lax.fori_loop(0, n, body, 0)


def _row_gather_wait(src_hbm, buf, sem, n):
    def body(r, c):
        _row_copy(src_hbm, 0, buf, r, sem).wait()
        return c

    lax.fori_loop(0, n, body, 0)


def _experts_body(be_ref, nu_ref, tok_ref, hn_hbm, wg_ref, wu_ref, wd_ref, y_ref, xbuf, sems, *, tb):
    del be_ref
    i = pl.program_id(0)
    n_used = nu_ref[0]
    slot = i % 2

    @pl.when(i == 0)
    def _():
        _row_gather_start(hn_hbm, tok_ref, 0, xbuf.at[0], 0, sems.at[0], tb)

    @pl.when(i + 1 < n_used)
    def _():
        _row_gather_start(hn_hbm, tok_ref, (i + 1) * tb, xbuf.at[1 - slot], 0, sems.at[1 - slot], tb)

    @pl.when(i < n_used)
    def _():
        _row_gather_wait(hn_hbm, xbuf.at[slot], sems.at[slot], tb)
        x = xbuf[slot].astype(BF16)
        hg = jnp.dot(x, wg_ref[0], preferred_element_type=F32)
        hu = jnp.dot(x, wu_ref[0], preferred_element_type=F32)
        hid = (hg * jax.nn.sigmoid(hg) * hu).astype(BF16)
        y_ref[...] = jnp.dot(hid, wd_ref[0], preferred_element_type=F32)

    @pl.when(i >= n_used)
    def _():
        y_ref[...] = jnp.zeros_like(y_ref)


def _experts(blk_expert, n_used, row_tok, hn, wg, wu, wd, *, tb):
    n_blocks = blk_expert.shape[0]
    wmap = lambda i, be, nu, tok: (be[i], 0, 0)
    grid_spec = pltpu.PrefetchScalarGridSpec(
        num_scalar_prefetch=3,
        grid=(n_blocks,),
        in_specs=[pl.BlockSpec(memory_space=pl.ANY),
                  pl.BlockSpec((1, D_MODEL, D_EXPERT), wmap),
                  pl.BlockSpec((1, D_MODEL, D_EXPERT), wmap),
                  pl.BlockSpec((1, D_EXPERT, D_MODEL), wmap)],
        out_specs=pl.BlockSpec((tb, D_MODEL), lambda i, be, nu, tok: (i, 0)),
        scratch_shapes=[pltpu.VMEM((2, tb, D_MODEL), F32), pltpu.SemaphoreType.DMA((2,))],
    )
    return pl.pallas_call(
        functools.partial(_experts_body, tb=tb),
        grid_spec=grid_spec,
        out_shape=_sds((n_blocks * tb, D_MODEL), F32),
        compiler_params=_cparams(("arbitrary",)),
        name="moe_experts",
    )(blk_expert, n_used, row_tok, hn, wg, wu, wd)


def _combine_body(dest_ref, x_ref, iw_ref, y_hbm, o_ref, ybuf, sems, *, tm, n):
    i = pl.program_id(0)
    n_steps = pl.num_programs(0)
    slot = i % 2

    def start(step, to_slot):
        for k in range(2):
            _row_gather_start(y_hbm, dest_ref, k * n + step * tm, ybuf.at[to_slot], k * tm,
                              sems.at[to_slot], tm)

    @pl.when(i == 0)
    def _():
        start(0, 0)

    @pl.when(i + 1 < n_steps)
    def _():
        start(i + 1, 1 - slot)

    _row_gather_wait(y_hbm, ybuf.at[slot], sems.at[slot], 2 * tm)
    iw = iw_ref[...]
    o_ref[...] = (x_ref[...] + iw[:, 0:1] * ybuf[slot, 0:tm, :] + iw[:, 1:2] * ybuf[slot, tm:2 * tm, :])


def _combine(dest_flat, x1, iw, y, *, tm):
    n = x1.shape[0]
    grid_spec = pltpu.PrefetchScalarGridSpec(
        num_scalar_prefetch=1,
        grid=(n // tm,),
        in_specs=[pl.BlockSpec((tm, D_MODEL), lambda i, d: (i, 0)),
                  pl.BlockSpec((tm, LANES), lambda i, d: (i, 0)),
                  pl.BlockSpec(memory_space=pl.ANY)],
        out_specs=pl.BlockSpec((tm, D_MODEL), lambda i, d: (i, 0)),
        scratch_shapes=[pltpu.VMEM((2, 2 * tm, D_MODEL), F32), pltpu.SemaphoreType.DMA((2,))],
    )
    return pl.pallas_call(
        functools.partial(_combine_body, tm=tm, n=n),
        grid_spec=grid_spec,
        out_shape=_sds((n, D_MODEL), F32),
        compiler_params=_cparams(("arbitrary",)),
        name="moe_combine",
    )(dest_flat, x1, iw, y)


def _moe(x1, norm_g, wr, wg, wu, wd, *, tm, tb):
    n = x1.shape[0]
    hn, ii, iw, cnt = _router(x1, norm_g, wr, tm=tm)
    counts = cnt[0, :N_EXPERTS]
    padded = (counts + tb - 1) // tb * tb
    pend = jnp.cumsum(padded)
    pstart = pend - padded
    n_blocks = -(-2 * n // tb) + N_EXPERTS
    eids = jnp.arange(N_EXPERTS, dtype=I32)
    e12 = ii[:, 0:2]
    start12 = jnp.sum(jnp.where(e12[:, :, None] == eids, pstart, 0), axis=-1)
    dest_flat = (start12 + ii[:, 2:4]).T.reshape(-1).astype(I32)
    blk_start = jnp.arange(n_blocks, dtype=I32) * tb
    blk_expert = jnp.minimum(jnp.sum(blk_start[:, None] >= pend[None, :], axis=-1), N_EXPERTS - 1).astype(I32)
    n_used = (pend[-1:] // tb).astype(I32)
    row_tok = _row_token(dest_flat, n_rows=n_blocks * tb)
    y = _experts(blk_expert, n_used, row_tok, hn, wg, wu, wd, tb=tb)
    return _combine(dest_flat, x1, iw, y, tm=tm)


def _rope_tables(pos):
    half = ROT_DIM // 2
    inv = jnp.power(ROPE_THETA, -jnp.arange(0, ROT_DIM, 2, dtype=F32) / ROT_DIM)
    ang = pos.astype(F32)[:, None] * inv[None, :]
    cos, sin = jnp.cos(ang), jnp.sin(ang)
    t = pos.shape[0]
    rest = HEAD_DIM - ROT_DIM
    cos_t = jnp.concatenate([cos, cos, jnp.ones((t, rest), F32)], axis=1)
    sa = jnp.concatenate([jnp.zeros((t, half), F32), sin, jnp.zeros((t, rest), F32)], axis=1)
    sb = jnp.concatenate([-sin, jnp.zeros((t, half + rest), F32)], axis=1)
    return cos_t, sa, sb


def _head_match_mask(rows, cols):
    r = np.arange(rows)[:, None]
    c = np.arange(cols)[None, :]
    return np.where((c % N_KV) == ((r % N_HEADS) // GROUP), 0.0, NEG).astype(np.float32)


def _prep_weights(w_in, b_forget, gains, w_branch_fox, w_branch_moba, w_out, w_router_group,
                  w_router_expert, w_exp_gate, w_exp_up, w_exp_down):
    fq, fk, mq, mk = gains
    off_f = Q_W + 2 * KV_W
    w_main = jnp.concatenate([w_in[:, :off_f], w_in[:, off_f + N_HEADS:]], axis=1).astype(BF16)
    w_f = jnp.pad(w_in[:, off_f:off_f + N_HEADS], ((0, 0), (0, LANES - N_HEADS))).astype(BF16)
    b_f = jnp.pad(b_forget, (0, LANES - N_HEADS))[None, :]
    ones = lambda w: jnp.ones((w,), F32)
    colgain = jnp.concatenate([
        jnp.tile(fq, N_HEADS), jnp.tile(fk, N_KV), ones(KV_W),
        jnp.tile(mq, N_HEADS), jnp.tile(mk, N_KV), ones(KV_W), ones(2 * D_MODEL)])[None, :]
    wr = jnp.pad(jnp.concatenate([w_router_group, w_router_expert], axis=1),
                 ((0, 0), (0, LANES - N_GROUPS - N_EXPERTS)))
    return dict(w_main=w_main, w_f=w_f, b_f=b_f, colgain=colgain,
                wa=w_branch_fox.astype(BF16), wb=w_branch_moba.astype(BF16), wo=w_out.astype(BF16),
                wr=wr, wg=w_exp_gate.astype(BF16), wu=w_exp_up.astype(BF16), wd=w_exp_down.astype(BF16))


def _pick(n, pref):
    t = min(n, pref)
    while n % t:
        t //= 2
    return t


def _prompt_layer(x, wts, norm_mix, norm_ffn):
    b, t, _ = x.shape
    n = b * t
    x2 = x.reshape(n, D_MODEL)
    tm = _pick(t, 1024)
    tabs = _rope_tables(jnp.arange(t, dtype=I32))
    proj, logf = _inproj(x2, norm_mix[None, :], wts["w_main"], wts["w_f"], wts["b_f"], wts["colgain"],
                         tabs, tm=tm)
    proj3 = proj.reshape(b, t, PROJ_W)
    c3 = _cumsum(logf.reshape(b, t, LANES), tc=_pick(t, 256))
    ck_t = jnp.transpose(c3[:, :, :N_HEADS], (0, 2, 1))
    oa = _fox_prompt(proj3, c3, ck_t, tq=_pick(t, 512))
    ob = _moba_prompt(proj3)
    x1 = _outproj(oa.reshape(n, Q_W), ob.reshape(n, Q_W), proj, x2, wts["wa"], wts["wb"], wts["wo"],
                  tm=_pick(n, 256))
    y = _moe(x1, norm_ffn[None, :], wts["wr"], wts["wg"], wts["wu"], wts["wd"], tm=_pick(n, 256), tb=256)
    return y.reshape(b, t, D_MODEL), proj3, logf.reshape(b, t, LANES)[:, :, :N_HEADS]


def _decode_layer(x, wts, norm_mix, norm_ffn, caches, page_table):
    b, t, _ = x.shape
    n = b * t
    n_pages = page_table.shape[1]
    past = n_pages * PAGE
    assert past % MOBA_BLOCK == 0 and t <= MOBA_BLOCK and (t * N_HEADS) % 8 == 0
    cache_fk, cache_fv, cache_logf, cache_mk, cache_mv = caches
    n_pool = cache_fk.shape[0]
    x2 = x.reshape(n, D_MODEL)
    pos = past + jnp.arange(t, dtype=I32)
    tabs = tuple(jnp.tile(tb_, (b, 1)) for tb_ in _rope_tables(pos))
    proj, logf = _inproj(x2, norm_mix[None, :], wts["w_main"], wts["w_f"], wts["b_f"], wts["colgain"],
                         tabs, tm=n)
    pt_flat = page_table.reshape(-1).astype(I32)
    rows = t * N_HEADS
    rows_new = 8
    logf_new = jnp.pad(logf.reshape(b, t, LANES), ((0, 0), (0, rows_new - t), (0, 0)))
    c_past, c_new = _paged_cumsum(pt_flat, cache_logf, logf_new, n_pages=n_pages, pages=_pick(n_pages, 8))
    c_new = c_new[:, :t, :]
    nck_rep = jnp.repeat(-jnp.transpose(c_past, (0, 2, 1)), N_KV, axis=-1)
    cq32 = c_new.reshape(b, rows, 1)
    r_t = np.arange(rows)[:, None] // N_HEADS
    c_t = np.arange(t * N_KV)[None, :] // N_KV
    new_mask = _head_match_mask(rows, t * N_KV) + np.where(c_t <= r_t, 0.0, NEG).astype(np.float32)
    nck_new = jnp.repeat(-jnp.transpose(c_new, (0, 2, 1)), N_KV, axis=-1)
    bias_new = jnp.tile(nck_new, (1, t, 1)) + new_mask[None]
    page_mask = jnp.asarray(_head_match_mask(rows, PAGE * N_KV))
    blk_mask = jnp.asarray(_head_match_mask(rows, MOBA_BLOCK * N_KV))

    def q_rows(col):
        return proj[:, col:col + Q_W].reshape(b, rows, HEAD_DIM)

    def kv_rows(col):
        return proj[:, col:col + KV_W].reshape(b, t * N_KV, HEAD_DIM)

    page_rows = lambda c: c.reshape(n_pool, PAGE * N_KV, HEAD_DIM)
    oa = _fox_decode(pt_flat, q_rows(COL_FOX_Q), cq32, nck_rep, page_mask, kv_rows(COL_FOX_K),
                     kv_rows(COL_FOX_V), bias_new, page_rows(cache_fk), page_rows(cache_fv),
                     n_pages=n_pages, pages=_pick(n_pages, 4))
    ob = _moba_decode(pt_flat, q_rows(COL_MOBA_Q), blk_mask, kv_rows(COL_MOBA_K), kv_rows(COL_MOBA_V),
                      jnp.asarray(new_mask), page_rows(cache_mk), page_rows(cache_mv),
                      n_pages=n_pages, nb=_pick(n_pages * PAGE // MOBA_BLOCK, 2))
    x1 = _outproj(oa.reshape(n, Q_W), ob.reshape(n, Q_W), proj, x2, wts["wa"], wts["wb"], wts["wo"], tm=n)
    y = _moe(x1, norm_ffn[None, :], wts["wr"], wts["wg"], wts["wu"], wts["wd"], tm=n, tb=16)
    return y.reshape(b, t, D_MODEL), proj.reshape(b, t, PROJ_W), logf.reshape(b, t, LANES)[:, :, :N_HEADS]


def _kv_outputs(proj3, logf):
    b, t, _ = proj3.shape
    kv = lambda col: proj3[:, :, col:col + KV_W].reshape(1, b, t, N_KV, HEAD_DIM)
    return kv(COL_FOX_K), kv(COL_FOX_V), logf[None], kv(COL_MOBA_K), kv(COL_MOBA_V)


def kernel(x_prompt, x_sample, cache_fox_k, cache_fox_v, cache_fox_logf, cache_moba_k, cache_moba_v,
           page_table, norm_mix, w_in, b_forget, fox_q_norm, fox_k_norm, moba_q_norm, moba_k_norm,
           w_branch_fox, w_branch_moba, w_out, norm_ffn, w_router_group, w_router_expert,
           w_exp_gate, w_exp_up, w_exp_down):
    assert w_in.shape[0] == 1, "single-layer trunk"
    wts = _prep_weights(w_in[0], b_forget[0], (fox_q_norm[0], fox_k_norm[0], moba_q_norm[0], moba_k_norm[0]),
                        w_branch_fox[0], w_branch_moba[0], w_out[0], w_router_group[0], w_router_expert[0],
                        w_exp_gate[0], w_exp_up[0], w_exp_down[0])
    yp, proj_p, logf_p = _prompt_layer(x_prompt, wts, norm_mix[0], norm_ffn[0])
    caches = (cache_fox_k[0], cache_fox_v[0], cache_fox_logf[0], cache_moba_k[0], cache_moba_v[0])
    ys, proj_s, logf_s = _decode_layer(x_sample, wts, norm_mix[0], norm_ffn[0], caches, page_table)
    return (yp, ys) + _kv_outputs(proj_p, logf_p) + _kv_outputs(proj_s, logf_s)
```

```python
import functools

import numpy as np
import jax
import jax.numpy as jnp
from jax import lax
from jax.experimental import pallas as pl
from jax.experimental.pallas import tpu as pltpu

F32 = jnp.float32
BF16 = jnp.bfloat16
I32 = jnp.int32
HIGHEST = lax.Precision.HIGHEST

LANES = 128
SUBLANES = 8
D_MODEL = 2048
HEAD_DIM = 128
N_HEADS = 8
N_KV = 4
GROUP = N_HEADS // N_KV
ROT_DIM = HEAD_DIM // 4
ROPE_THETA = 500000.0
ATTN_SCALE = HEAD_DIM ** -0.5
MOBA_BLOCK = 256
MOBA_TOPK = 3
PAGE = 128
PAGE_ROWS = PAGE * N_KV
N_GROUPS = 4
EXPERTS_PER_GROUP = 8
N_EXPERTS = N_GROUPS * EXPERTS_PER_GROUP
D_EXPERT = 512
RMS_EPS = 1e-6
NEG = -1e30
Q_W = N_HEADS * HEAD_DIM
KV_W = N_KV * HEAD_DIM
PROJ_W = 2 * (Q_W + 2 * KV_W) + 2 * D_MODEL
PROJ_TILE = 512
COL_FOX_Q, COL_FOX_K, COL_FOX_V = 0, Q_W, Q_W + KV_W
COL_MOBA_Q = Q_W + 2 * KV_W
COL_MOBA_K, COL_MOBA_V = COL_MOBA_Q + Q_W, COL_MOBA_Q + Q_W + KV_W
COL_GA = 2 * (Q_W + 2 * KV_W)
COL_GB = COL_GA + D_MODEL
VMEM_LIMIT = 56 * 1024 * 1024


def _cparams(sem):
    return pltpu.CompilerParams(dimension_semantics=sem, vmem_limit_bytes=VMEM_LIMIT)


def _sds(shape, dtype):
    return jax.ShapeDtypeStruct(shape, dtype)


def _head_rmsnorm(y, gain):
    ms = jnp.mean(y * y, axis=-1, keepdims=True)
    return y * lax.rsqrt(ms + RMS_EPS) * gain


def _inproj_body(x_ref, g_ref, w_ref, wf_ref, bf_ref, cg_ref, cos_ref, sa_ref, sb_ref,
                 proj_ref, logf_ref, h_scr):
    j = pl.program_id(1)

    @pl.when(j == 0)
    def _():
        x = x_ref[...]
        ms = jnp.mean(x * x, axis=-1, keepdims=True)
        h = (x * lax.rsqrt(ms + RMS_EPS) * g_ref[...]).astype(BF16)
        h_scr[...] = h
        z = jnp.dot(h, wf_ref[...], preferred_element_type=F32) + bf_ref[...]
        logf_ref[...] = -(jnp.maximum(-z, 0.0) + jnp.log1p(jnp.exp(-jnp.abs(z))))

    acc = jnp.dot(h_scr[...], w_ref[...], preferred_element_type=F32)
    fox_normed = (j >= COL_FOX_Q // PROJ_TILE) & (j < COL_FOX_V // PROJ_TILE)
    moba_normed = (j >= COL_MOBA_Q // PROJ_TILE) & (j < COL_MOBA_V // PROJ_TILE)

    @pl.when(jnp.logical_not(fox_normed | moba_normed))
    def _():
        proj_ref[...] = acc

    @pl.when(fox_normed)
    def _():
        for hh in range(PROJ_TILE // HEAD_DIM):
            sl = slice(hh * HEAD_DIM, (hh + 1) * HEAD_DIM)
            proj_ref[:, sl] = _head_rmsnorm(acc[:, sl], cg_ref[:, sl])

    @pl.when(moba_normed)
    def _():
        cos, sa, sb = cos_ref[...], sa_ref[...], sb_ref[...]
        for hh in range(PROJ_TILE // HEAD_DIM):
            sl = slice(hh * HEAD_DIM, (hh + 1) * HEAD_DIM)
            y = _head_rmsnorm(acc[:, sl], cg_ref[:, sl])
            half = ROT_DIM // 2
            y = (y * cos + pltpu.roll(y, half, axis=1) * sa
                 + pltpu.roll(y, HEAD_DIM - half, axis=1) * sb)
            proj_ref[:, sl] = y


def _inproj(x2, g, w_main, w_f, b_f, colgain, tabs, *, tm):
    n = x2.shape[0]
    ntab = tabs[0].shape[0] // tm
    tab_spec = pl.BlockSpec((tm, LANES), lambda i, j: (i % ntab, 0))
    return pl.pallas_call(
        _inproj_body,
        grid=(n // tm, PROJ_W // PROJ_TILE),
        in_specs=[
            pl.BlockSpec((tm, D_MODEL), lambda i, j: (i, 0)),
            pl.BlockSpec((1, D_MODEL), lambda i, j: (0, 0)),
            pl.BlockSpec((D_MODEL, PROJ_TILE), lambda i, j: (0, j)),
            pl.BlockSpec((D_MODEL, LANES), lambda i, j: (0, 0)),
            pl.BlockSpec((1, LANES), lambda i, j: (0, 0)),
            pl.BlockSpec((1, PROJ_TILE), lambda i, j: (0, j)),
            tab_spec, tab_spec, tab_spec,
        ],
        out_specs=[pl.BlockSpec((tm, PROJ_TILE), lambda i, j: (i, j)),
                   pl.BlockSpec((tm, LANES), lambda i, j: (i, 0))],
        out_shape=[_sds((n, PROJ_W), F32), _sds((n, LANES), F32)],
        scratch_shapes=[pltpu.VMEM((tm, D_MODEL), BF16)],
        compiler_params=_cparams(("parallel", "arbitrary")),
        name="inproj",
    )(x2, g, w_main, w_f, b_f, colgain, *tabs)


def _lower_tri(n, strict=False):
    r = lax.broadcasted_iota(I32, (n, n), 0)
    c = lax.broadcasted_iota(I32, (n, n), 1)
    return (c < r) if strict else (c <= r)


def _cumsum_body(x_ref, o_ref, carry):
    @pl.when(pl.program_id(1) == 0)
    def _():
        carry[...] = jnp.zeros_like(carry)

    x = x_ref[0]
    tc = x.shape[0]
    tri = _lower_tri(tc).astype(F32)
    cs = jnp.dot(tri, x, precision=HIGHEST, preferred_element_type=F32) + carry[...]
    o_ref[0] = cs
    carry[...] = cs[tc - 1:tc, :]


def _cumsum(logf3, *, tc):
    b, t, w = logf3.shape
    return pl.pallas_call(
        _cumsum_body,
        grid=(b, t // tc),
        in_specs=[pl.BlockSpec((1, tc, w), lambda i, j: (i, j, 0))],
        out_specs=pl.BlockSpec((1, tc, w), lambda i, j: (i, j, 0)),
        out_shape=_sds((b, t, w), F32),
        scratch_shapes=[pltpu.VMEM((1, w), F32)],
        compiler_params=_cparams(("parallel", "arbitrary")),
        name="logf_cumsum",
    )(logf3)


def _dot_exact01(x, m01):
    hi = x.astype(BF16)
    r1 = x - hi.astype(F32)
    mid = r1.astype(BF16)
    lo = (r1 - mid.astype(F32)).astype(BF16)
    return (jnp.dot(hi, m01, preferred_element_type=F32) + jnp.dot(mid, m01, preferred_element_type=F32)
            + jnp.dot(lo, m01, preferred_element_type=F32))


def _paged_cumsum_body(pt_ref, *refs, pages):
    del pt_ref
    page_refs = refs[:pages]
    new_ref, cum_rep_ref, cum_ref = refs[pages:pages + 3]
    o_ref, onew_ref, carry = refs[pages + 3:]
    s = pl.program_id(1)

    @pl.when(s == 0)
    def _():
        carry[...] = jnp.zeros_like(carry)

    x = jnp.concatenate([r[0] for r in page_refs], axis=0)
    w = _dot_exact01(x, cum_rep_ref[...])
    c = carry[...]
    for p in range(pages):
        cp = w[p * N_HEADS:(p + 1) * N_HEADS, :] + c
        o_ref[0, :, p * PAGE_ROWS:(p + 1) * PAGE_ROWS] = -cp
        c = cp[:, PAGE_ROWS - 1:PAGE_ROWS]
    carry[...] = c

    @pl.when(s == pl.num_programs(1) - 1)
    def _():
        onew_ref[0] = _dot_exact01(new_ref[0], cum_ref[...]) + c


def _paged_cumsum(pt_flat, cache_logf_t, logf_new_t, *, n_pages, pages):
    b = logf_new_t.shape[0]
    upper = np.triu(np.ones((PAGE, PAGE), np.float32))
    cum = jnp.asarray(upper, BF16)
    cum_rep = jnp.asarray(np.repeat(upper, N_KV, axis=1), BF16)

    def page_map(p):
        return lambda i, s, pt: (pt[i * n_pages + s * pages + p], 0, 0)

    const = lambda i, s, pt: (0, 0)
    grid_spec = pltpu.PrefetchScalarGridSpec(
        num_scalar_prefetch=1,
        grid=(b, n_pages // pages),
        in_specs=[pl.BlockSpec((1, N_HEADS, PAGE), page_map(p)) for p in range(pages)]
        + [pl.BlockSpec((1, N_HEADS, LANES), lambda i, s, pt: (i, 0, 0)),
           pl.BlockSpec((PAGE, PAGE_ROWS), const), pl.BlockSpec((PAGE, PAGE), const)],
        out_specs=[pl.BlockSpec((1, N_HEADS, pages * PAGE_ROWS), lambda i, s, pt: (i, 0, s)),
                   pl.BlockSpec((1, N_HEADS, LANES), lambda i, s, pt: (i, 0, 0))],
        scratch_shapes=[pltpu.VMEM((N_HEADS, 1), F32)],
    )
    return pl.pallas_call(
        functools.partial(_paged_cumsum_body, pages=pages),
        grid_spec=grid_spec,
        out_shape=[_sds((b, N_HEADS, n_pages * PAGE_ROWS), F32), _sds((b, N_HEADS, LANES), F32)],
        compiler_params=_cparams(("parallel", "arbitrary")),
        name="paged_logf_cumsum",
    )(pt_flat, *([cache_logf_t] * pages), logf_new_t, cum_rep, cum)


def _softmax_tile(t, rowshift, v, carry):
    m, l, acc = carry
    m_new = jnp.maximum(m, jnp.max(t, axis=-1, keepdims=True) + rowshift)
    p = jnp.exp(t + (rowshift - m_new))
    alpha = jnp.exp(m - m_new)
    l = alpha * l + jnp.sum(p, axis=-1, keepdims=True)
    acc = alpha * acc + jnp.dot(p.astype(BF16), v, preferred_element_type=F32)
    return m_new, l, acc


def _softmax_init(rows):
    return (jnp.full((rows, 1), NEG, F32), jnp.zeros((rows, 1), F32), jnp.zeros((rows, HEAD_DIM), F32))


def _qk(q, k):
    return lax.dot_general(q, k, (((1,), (1,)), ((), ())), preferred_element_type=F32)


def _fox_body(q_ref, k_ref, v_ref, cq_ref, ck_ref, o_ref, *, tq):
    h = pl.program_id(1)
    qi = pl.program_id(2)
    lane = lax.broadcasted_iota(I32, (tq, LANES), 1)
    qs, cqs = [], []
    for g in range(GROUP):
        qs.append((q_ref[0, :, g * HEAD_DIM:(g + 1) * HEAD_DIM] * ATTN_SCALE).astype(BF16))
        cqs.append(jnp.sum(jnp.where(lane == h * GROUP + g, cq_ref[0], 0.0), axis=-1, keepdims=True))

    def tile(start, diag, carries):
        k = k_ref[0, pl.ds(start, tq), :].astype(BF16)
        v = v_ref[0, pl.ds(start, tq), :].astype(BF16)
        out = []
        for g in range(GROUP):
            ck = ck_ref[0, pl.ds(h * GROUP + g, 1), pl.ds(start, tq)]
            t = _qk(qs[g], k) - ck
            if diag:
                t = jnp.where(_lower_tri(tq), t, NEG)
            out.append(_softmax_tile(t, cqs[g], v, carries[g]))
        return tuple(out)

    carries = tile(pl.multiple_of(qi * tq, tq), True, (_softmax_init(tq),) * GROUP)
    carries = lax.fori_loop(0, qi, lambda j, c: tile(pl.multiple_of(j * tq, tq), False, c), carries)
    for g in range(GROUP):
        _, l, acc = carries[g]
        o_ref[0, :, g * HEAD_DIM:(g + 1) * HEAD_DIM] = (acc / l).astype(o_ref.dtype)


def _fox_prompt(proj3, c3, ck_t, *, tq):
    b, t, _ = proj3.shape
    gw = GROUP * HEAD_DIM
    qb, kb, vb = COL_FOX_Q // gw, COL_FOX_K // HEAD_DIM, COL_FOX_V // HEAD_DIM
    return pl.pallas_call(
        functools.partial(_fox_body, tq=tq),
        grid=(b, N_KV, t // tq),
        in_specs=[
            pl.BlockSpec((1, tq, gw), lambda i, h, q: (i, q, qb + h)),
            pl.BlockSpec((1, t, HEAD_DIM), lambda i, h, q: (i, 0, kb + h)),
            pl.BlockSpec((1, t, HEAD_DIM), lambda i, h, q: (i, 0, vb + h)),
            pl.BlockSpec((1, tq, LANES), lambda i, h, q: (i, q, 0)),
            pl.BlockSpec((1, N_HEADS, t), lambda i, h, q: (i, 0, 0)),
        ],
        out_specs=pl.BlockSpec((1, tq, gw), lambda i, h, q: (i, q, h)),
        out_shape=_sds((b, t, Q_W), BF16),
        compiler_params=_cparams(("parallel", "parallel", "arbitrary")),
        name="fox_prompt",
    )(proj3, proj3, proj3, c3, ck_t)


def _topk_select(g, elig, n_cand, axis):
    idx = lax.broadcasted_iota(I32, g.shape, axis)
    g = jnp.where(elig, g, -jnp.inf)
    cnt = jnp.zeros(g.shape, F32)
    for m in range(n_cand):
        gm = g[m:m + 1, :] if axis == 0 else g[:, m:m + 1]
        beats = (gm > g) | ((gm == g) & (m < idx))
        cnt = cnt + jnp.where(beats, 1.0, 0.0)
    return elig & (cnt < MOBA_TOPK)


def _moba_body(q_ref, k_ref, v_ref, o_ref, kmean_scr, *, n_blk):
    qi = pl.program_id(2)
    tq = MOBA_BLOCK
    nb8 = kmean_scr.shape[0]

    @pl.when(qi == 0)
    def _():
        kmean_scr[...] = jnp.zeros_like(kmean_scr)
        for n in range(n_blk):
            kb = k_ref[0, n * tq:(n + 1) * tq, :]
            kmean_scr[n:n + 1, :] = jnp.mean(kb, axis=0, keepdims=True)

    eye = jnp.where(lax.broadcasted_iota(I32, (tq, tq), 0) == lax.broadcasted_iota(I32, (tq, tq), 1),
                    1.0, 0.0).astype(BF16)
    blk = lax.broadcasted_iota(I32, (nb8, tq), 0)
    lane_blk = lax.broadcasted_iota(I32, (tq, nb8), 1)
    qs, sels = [], []
    for g in range(GROUP):
        qf = q_ref[0, :, g * HEAD_DIM:(g + 1) * HEAD_DIM]
        qs.append((qf * ATTN_SCALE).astype(BF16))
        gate_t = lax.dot_general(kmean_scr[...], qf, (((1,), (1,)), ((), ())),
                                 precision=HIGHEST, preferred_element_type=F32)
        sel_t = jnp.where(_topk_select(gate_t, blk < qi, n_blk, 0), 1.0, 0.0).astype(BF16)
        sels.append(_qk(eye, sel_t))

    def load(start):
        return (k_ref[0, pl.ds(start, tq), :].astype(BF16), v_ref[0, pl.ds(start, tq), :].astype(BF16))

    def past(n, carries):
        k, v = load(pl.multiple_of(n * tq, tq))
        out = []
        for g in range(GROUP):
            flag = jnp.sum(jnp.where(lane_blk == n, sels[g], 0.0), axis=-1, keepdims=True)
            t = _qk(qs[g], k) + (flag - 1.0) * (-NEG)
            out.append(_softmax_tile(t, 0.0, v, carries[g]))
        return tuple(out)

    k, v = load(pl.multiple_of(qi * tq, tq))
    carries = tuple(
        _softmax_tile(jnp.where(_lower_tri(tq), _qk(qs[g], k), NEG), 0.0, v, _softmax_init(tq))
        for g in range(GROUP))
    carries = lax.fori_loop(0, lax.shift_right_logical(qi, 1),
                            lambda j, c: past(2 * j + 1, past(2 * j, c)), carries)
    carries = lax.cond((qi & 1) == 1, lambda c: past(qi - 1, c), lambda c: c, carries)
    for g in range(GROUP):
        _, l, acc = carries[g]
        o_ref[0, :, g * HEAD_DIM:(g + 1) * HEAD_DIM] = (acc / l).astype(o_ref.dtype)


def _moba_prompt(proj3):
    b, t, _ = proj3.shape
    n_blk = t // MOBA_BLOCK
    assert t % MOBA_BLOCK == 0
    nb8 = -(-n_blk // SUBLANES) * SUBLANES
    gw = GROUP * HEAD_DIM
    qb, kb, vb = COL_MOBA_Q // gw, COL_MOBA_K // HEAD_DIM, COL_MOBA_V // HEAD_DIM
    tq = MOBA_BLOCK
    return pl.pallas_call(
        functools.partial(_moba_body, n_blk=n_blk),
        grid=(b, N_KV, t // tq),
        in_specs=[
            pl.BlockSpec((1, tq, gw), lambda i, h, q: (i, q, qb + h)),
            pl.BlockSpec((1, t, HEAD_DIM), lambda i, h, q: (i, 0, kb + h)),
            pl.BlockSpec((1, t, HEAD_DIM), lambda i, h, q: (i, 0, vb + h)),
        ],
        out_specs=pl.BlockSpec((1, tq, gw), lambda i, h, q: (i, q, h)),
        out_shape=_sds((b, t, Q_W), BF16),
        scratch_shapes=[pltpu.VMEM((nb8, HEAD_DIM), F32)],
        compiler_params=_cparams(("parallel", "parallel", "arbitrary")),
        name="moba_prompt",
    )(proj3, proj3, proj3)


def _fox_decode_body(pt_ref, *refs, pages, rows):
    del pt_ref
    q_ref, cq_ref, nck_ref, mask_ref, knew_ref, vnew_ref, bnew_ref = refs[:7]
    k_refs = refs[7:7 + pages]
    v_refs = refs[7 + pages:7 + 2 * pages]
    o_ref, m_scr, l_scr, acc_scr = refs[7 + 2 * pages:]
    s = pl.program_id(1)

    @pl.when(s == 0)
    def _():
        m_scr[...] = jnp.full_like(m_scr, NEG)
        l_scr[...] = jnp.zeros_like(l_scr)
        acc_scr[...] = jnp.zeros_like(acc_scr)

    q = (q_ref[0] * ATTN_SCALE).astype(BF16)
    cq = cq_ref[0]
    m, l, acc = m_scr[...], l_scr[...], acc_scr[...]
    t = jnp.concatenate([_qk(q, k_refs[p][0].astype(BF16)) for p in range(pages)], axis=1)
    t = t + jnp.concatenate([nck_ref[0]] * (rows // N_HEADS), axis=0) + mask_ref[...]
    m_new = jnp.maximum(m, jnp.max(t, axis=-1, keepdims=True) + cq)
    pf = jnp.exp(t + (cq - m_new))
    alpha = jnp.exp(m - m_new)
    l = alpha * l + jnp.sum(pf, axis=-1, keepdims=True)
    p_ = pf.astype(BF16)
    pv = jnp.dot(p_[:, 0:PAGE_ROWS], v_refs[0][0].astype(BF16), preferred_element_type=F32)
    for p in range(1, pages):
        pv = pv + jnp.dot(p_[:, p * PAGE_ROWS:(p + 1) * PAGE_ROWS], v_refs[p][0].astype(BF16),
                          preferred_element_type=F32)
    acc = alpha * acc + pv
    m_scr[...], l_scr[...], acc_scr[...] = m_new, l, acc

    @pl.when(s == pl.num_programs(1) - 1)
    def _():
        tn = _qk(q, knew_ref[0].astype(BF16)) + bnew_ref[0]
        _, l2, acc2 = _softmax_tile(tn, cq, vnew_ref[0].astype(BF16), (m_new, l, acc))
        o_ref[0] = (acc2 / l2).astype(o_ref.dtype)


def _fox_decode(pt_flat, q32, cq32, nck_rep, mask, k_new, v_new, bias_new, cache_k, cache_v,
                *, n_pages, pages):
    b, rows, _ = q32.shape
    n_new = k_new.shape[1]

    def page_map(p):
        return lambda i, s, pt: (pt[i * n_pages + s * pages + p], 0, 0)

    per_b = lambda i, s, pt: (i, 0, 0)
    grid_spec = pltpu.PrefetchScalarGridSpec(
        num_scalar_prefetch=1,
        grid=(b, n_pages // pages),
        in_specs=[
            pl.BlockSpec((1, rows, HEAD_DIM), per_b),
            pl.BlockSpec((1, rows, 1), per_b),
            pl.BlockSpec((1, N_HEADS, pages * PAGE_ROWS), lambda i, s, pt: (i, 0, s)),
            pl.BlockSpec((rows, pages * PAGE_ROWS), lambda i, s, pt: (0, 0)),
            pl.BlockSpec((1, n_new, HEAD_DIM), per_b),
            pl.BlockSpec((1, n_new, HEAD_DIM), per_b),
            pl.BlockSpec((1, rows, n_new), per_b),
        ] + [pl.BlockSpec((1, PAGE_ROWS, HEAD_DIM), page_map(p)) for p in range(pages)] * 2,
        out_specs=pl.BlockSpec((1, rows, HEAD_DIM), per_b),
        scratch_shapes=[pltpu.VMEM((rows, 1), F32), pltpu.VMEM((rows, 1), F32),
                        pltpu.VMEM((rows, HEAD_DIM), F32)],
    )
    return pl.pallas_call(
        functools.partial(_fox_decode_body, pages=pages, rows=rows),
        grid_spec=grid_spec,
        out_shape=_sds((b, rows, HEAD_DIM), BF16),
        compiler_params=_cparams(("parallel", "arbitrary")),
        name="fox_decode",
    )(pt_flat, q32, cq32, nck_rep, jnp.tile(mask, (1, pages)), k_new, v_new, bias_new,
      *([cache_k] * pages), *([cache_v] * pages))


def _tree_sum(x):
    while x.shape[0] > 1:
        half = x.shape[0] // 2
        x = x[:half] + x[half:]
    return x[0]


def _moba_decode_body(pt_ref, *refs, nb, n_past, rows):
    del pt_ref
    ppb = MOBA_BLOCK // PAGE
    pages = nb * ppb
    q_ref, mask_ref, knew_ref, vnew_ref, bnew_ref = refs[:5]
    k_refs = refs[5:5 + pages]
    v_refs = refs[5 + pages:5 + 2 * pages]
    o_ref, m_scr, l_scr, g_scr, acc_scr = refs[5 + 2 * pages:]
    s = pl.program_id(1)

    @pl.when(s == 0)
    def _():
        m_scr[...] = jnp.zeros_like(m_scr)
        l_scr[...] = jnp.zeros_like(l_scr)
        g_scr[...] = jnp.zeros_like(g_scr)

    qf = q_ref[0]
    q = (qf * ATTN_SCALE).astype(BF16)
    lane = lax.broadcasted_iota(I32, (rows, LANES), 1)
    m_all, l_all, g_all = m_scr[...], l_scr[...], g_scr[...]
    for j in range(nb):
        n = s * nb + j
        kf = jnp.concatenate([k_refs[j * ppb + p][0] for p in range(ppb)], axis=0)
        vf = jnp.concatenate([v_refs[j * ppb + p][0] for p in range(ppb)], axis=0)
        t = _qk(q, kf.astype(BF16)) + mask_ref[...]
        m_n = jnp.max(t, axis=-1, keepdims=True)
        p_ = jnp.exp(t - m_n)
        l_n = jnp.sum(p_, axis=-1, keepdims=True)
        acc_scr[n] = jnp.dot(p_.astype(BF16), vf.astype(BF16), preferred_element_type=F32)
        s8 = _tree_sum(kf.reshape(MOBA_BLOCK * N_KV // SUBLANES, SUBLANES, HEAD_DIM))
        ks = s8[0:N_KV] + s8[N_KV:2 * N_KV]
        ks_h = jnp.concatenate([ks[h // GROUP:h // GROUP + 1] for h in range(N_HEADS)], axis=0)
        ks_r = jnp.concatenate([ks_h] * (rows // N_HEADS), axis=0)
        g_n = jnp.sum(qf * ks_r, axis=-1, keepdims=True) * (1.0 / MOBA_BLOCK)
        m_all = jnp.where(lane == n, m_n, m_all)
        l_all = jnp.where(lane == n, l_n, l_all)
        g_all = jnp.where(lane == n, g_n, g_all)
    m_scr[...], l_scr[...], g_scr[...] = m_all, l_all, g_all

    @pl.when(s == pl.num_programs(1) - 1)
    def _():
        t = _qk(q, knew_ref[0].astype(BF16)) + bnew_ref[...]
        m_own = jnp.max(t, axis=-1, keepdims=True)
        p_own = jnp.exp(t - m_own)
        l_own = jnp.sum(p_own, axis=-1, keepdims=True)
        acc_own = jnp.dot(p_own.astype(BF16), vnew_ref[0].astype(BF16), preferred_element_type=F32)
        sel = _topk_select(g_all, lane < n_past, n_past, 1)
        m_x = jnp.maximum(m_own, jnp.max(jnp.where(sel, m_all, NEG), axis=-1, keepdims=True))
        w = jnp.where(sel, jnp.exp(m_all - m_x), 0.0)
        w_own = jnp.exp(m_own - m_x)
        den = w_own * l_own + jnp.sum(w * l_all, axis=-1, keepdims=True)
        num = w_own * acc_own
        for n in range(n_past):
            num = num + w[:, n:n + 1] * acc_scr[n]
        o_ref[0] = (num / den).astype(o_ref.dtype)


def _moba_decode(pt_flat, q32, mask, k_new, v_new, bias_new, cache_k, cache_v, *, n_pages, nb):
    b, rows, _ = q32.shape
    ppb = MOBA_BLOCK // PAGE
    pages = nb * ppb
    n_past = n_pages // ppb
    assert n_past <= LANES and n_past % nb == 0
    n_new = k_new.shape[1]

    def page_map(p):
        return lambda i, s, pt: (pt[i * n_pages + s * pages + p], 0, 0)

    per_b = lambda i, s, pt: (i, 0, 0)
    grid_spec = pltpu.PrefetchScalarGridSpec(
        num_scalar_prefetch=1,
        grid=(b, n_past // nb),
        in_specs=[
            pl.BlockSpec((1, rows, HEAD_DIM), per_b),
            pl.BlockSpec((rows, ppb * PAGE_ROWS), lambda i, s, pt: (0, 0)),
            pl.BlockSpec((1, n_new, HEAD_DIM), per_b),
            pl.BlockSpec((1, n_new, HEAD_DIM), per_b),
            pl.BlockSpec((rows, n_new), lambda i, s, pt: (0, 0)),
        ] + [pl.BlockSpec((1, PAGE_ROWS, HEAD_DIM), page_map(p)) for p in range(pages)] * 2,
        out_specs=pl.BlockSpec((1, rows, HEAD_DIM), per_b),
        scratch_shapes=[pltpu.VMEM((rows, LANES), F32), pltpu.VMEM((rows, LANES), F32),
                        pltpu.VMEM((rows, LANES), F32), pltpu.VMEM((n_past, rows, HEAD_DIM), F32)],
    )
    return pl.pallas_call(
        functools.partial(_moba_decode_body, nb=nb, n_past=n_past, rows=rows),
        grid_spec=grid_spec,
        out_shape=_sds((b, rows, HEAD_DIM), BF16),
        compiler_params=_cparams(("parallel", "arbitrary")),
        name="moba_decode",
    )(pt_flat, q32, mask, k_new, v_new, bias_new, *([cache_k] * pages), *([cache_v] * pages))


def _outproj_body(oa_ref, ob_ref, ga_ref, gb_ref, x_ref, wa_ref, wb_ref, wo_ref, o_ref):
    ya = jnp.dot(oa_ref[...], wa_ref[...], preferred_element_type=F32)
    yb = jnp.dot(ob_ref[...], wb_ref[...], preferred_element_type=F32)
    mixed = jax.nn.sigmoid(ga_ref[...]) * ya + jax.nn.sigmoid(gb_ref[...]) * yb
    o_ref[...] = x_ref[...] + jnp.dot(mixed.astype(BF16), wo_ref[...], preferred_element_type=F32)


def _outproj(oa, ob, proj, x2, wa, wb, wo, *, tm):
    n = x2.shape[0]
    const = lambda i: (0, 0)
    once = pl.Buffered(1)
    return pl.pallas_call(
        _outproj_body,
        grid=(n // tm,),
        in_specs=[
            pl.BlockSpec((tm, Q_W), lambda i: (i, 0)),
            pl.BlockSpec((tm, Q_W), lambda i: (i, 0)),
            pl.BlockSpec((tm, D_MODEL), lambda i: (i, COL_GA // D_MODEL)),
            pl.BlockSpec((tm, D_MODEL), lambda i: (i, COL_GB // D_MODEL)),
            pl.BlockSpec((tm, D_MODEL), lambda i: (i, 0)),
            pl.BlockSpec((Q_W, D_MODEL), const, pipeline_mode=once),
            pl.BlockSpec((Q_W, D_MODEL), const, pipeline_mode=once),
            pl.BlockSpec((D_MODEL, D_MODEL), const, pipeline_mode=once),
        ],
        out_specs=pl.BlockSpec((tm, D_MODEL), lambda i: (i, 0)),
        out_shape=_sds((n, D_MODEL), F32),
        compiler_params=_cparams(("parallel",)),
        name="outproj",
    )(oa, ob, proj, proj, x2, wa, wb, wo)


def _first_lane(hit, lane):
    return jnp.min(jnp.where(hit, lane.astype(F32), float(LANES)), axis=-1, keepdims=True).astype(I32)


def _router_body(x_ref, g_ref, wr_ref, hn_ref, ii_ref, iw_ref, cnt_ref, cnt_scr):
    @pl.when(pl.program_id(0) == 0)
    def _():
        cnt_scr[...] = jnp.zeros_like(cnt_scr)

    x = x_ref[...]
    tm = x.shape[0]
    ms = jnp.mean(x * x, axis=-1, keepdims=True)
    hn = x * lax.rsqrt(ms + RMS_EPS) * g_ref[...]
    hn_ref[...] = hn
    h_hi = hn.astype(BF16)
    h_lo = (hn - h_hi.astype(F32)).astype(BF16)
    quad = jnp.dot(jnp.concatenate([h_hi, h_lo], axis=0), wr_ref[...], preferred_element_type=F32)
    logits = (quad[:tm, :LANES] + quad[:tm, LANES:]) + (quad[tm:, :LANES] + quad[tm:, LANES:])
    lane = lax.broadcasted_iota(I32, (tm, LANES), 1)
    is_g = lane < N_GROUPS
    gl = jnp.where(is_g, logits, -jnp.inf)
    gmax = jnp.max(gl, axis=-1, keepdims=True)
    grp = _first_lane(gl == gmax, lane)
    p_grp = 1.0 / jnp.sum(jnp.where(is_g, jnp.exp(gl - gmax), 0.0), axis=-1, keepdims=True)
    e_lane = lane - N_GROUPS
    shift = EXPERTS_PER_GROUP.bit_length() - 1
    in_grp = (e_lane < N_EXPERTS) & (lax.shift_right_arithmetic(e_lane, shift) == grp)
    el = jnp.where(in_grp, logits, -jnp.inf)
    v1 = jnp.max(el, axis=-1, keepdims=True)
    i1 = _first_lane(el == v1, lane)
    el2 = jnp.where(lane == i1, -jnp.inf, el)
    v2 = jnp.max(el2, axis=-1, keepdims=True)
    i2 = _first_lane(el2 == v2, lane)
    r = jnp.exp(v2 - v1)
    w1 = p_grp / (1.0 + r)
    w2 = p_grp * r / (1.0 + r)
    e1 = i1 - N_GROUPS
    e2 = i2 - N_GROUPS
    oh = jnp.where((lane == e1) | (lane == e2), 1.0, 0.0)
    before = cnt_scr[...] + jnp.dot(_lower_tri(tm, strict=True).astype(BF16), oh.astype(BF16),
                                    preferred_element_type=F32)
    r1 = jnp.sum(jnp.where(lane == e1, before, 0.0), axis=-1, keepdims=True)
    r2 = jnp.sum(jnp.where(lane == e2, before, 0.0), axis=-1, keepdims=True)
    cnt_scr[...] = cnt_scr[...] + jnp.sum(oh, axis=0, keepdims=True)
    cnt_ref[...] = cnt_scr[...].astype(I32)
    ii = jnp.where(lane == 0, e1, jnp.where(lane == 1, e2, jnp.where(
        lane == 2, r1.astype(I32), jnp.where(lane == 3, r2.astype(I32), 0))))
    ii_ref[...] = ii
    iw_ref[...] = jnp.where(lane == 0, w1, jnp.where(lane == 1, w2, 0.0))


def _router(x1, g, wr2, *, tm):
    n = x1.shape[0]
    return pl.pallas_call(
        _router_body,
        grid=(n // tm,),
        in_specs=[pl.BlockSpec((tm, D_MODEL), lambda i: (i, 0)),
                  pl.BlockSpec((1, D_MODEL), lambda i: (0, 0)),
                  pl.BlockSpec((D_MODEL, 2 * LANES), lambda i: (0, 0))],
        out_specs=[pl.BlockSpec((tm, D_MODEL), lambda i: (i, 0)),
                   pl.BlockSpec((tm, LANES), lambda i: (i, 0)),
                   pl.BlockSpec((tm, LANES), lambda i: (i, 0)),
                   pl.BlockSpec((1, LANES), lambda i: (0, 0))],
        out_shape=[_sds((n, D_MODEL), F32), _sds((n, LANES), I32), _sds((n, LANES), F32),
                   _sds((1, LANES), I32)],
        scratch_shapes=[pltpu.VMEM((1, LANES), F32)],
        compiler_params=_cparams(("arbitrary",)),
        name="moe_router",
    )(x1, g, wr2)


def _row_token_body(dest_ref, tok_ref, *, n_tok, n_rows):
    def zero(i, c):
        tok_ref[i] = 0
        return c

    lax.fori_loop(0, n_rows, zero, 0, unroll=8)

    def put(t, c):
        tok_ref[dest_ref[t]] = t
        tok_ref[dest_ref[n_tok + t]] = t
        return c

    lax.fori_loop(0, n_tok, put, 0, unroll=8)


def _row_token(dest_flat, *, n_rows):
    n_tok = dest_flat.shape[0] // 2
    return pl.pallas_call(
        functools.partial(_row_token_body, n_tok=n_tok, n_rows=n_rows),
        in_specs=[pl.BlockSpec(memory_space=pltpu.SMEM)],
        out_specs=pl.BlockSpec(memory_space=pltpu.SMEM),
        out_shape=_sds((n_rows,), I32),
        name="moe_row_token",
    )(dest_flat)


GATHER_UNROLL = 8


def _row_gather_start(src_hbm, idx_ref, base, buf, row0, sem, n):
    def body(r, c):
        pltpu.make_async_copy(src_hbm.at[pl.ds(idx_ref[base + r], 1)], buf.at[pl.ds(row0 + r, 1)],
                              sem).start()
        return c

    lax.fori_loop(0, n, body, 0, unroll=GATHER_UNROLL)


def _row_gather_wait(src_hbm, buf, sem):
    pltpu.make_async_copy(src_hbm.at[pl.ds(0, buf.shape[0])], buf, sem).wait()


def _experts_body(be_ref, nu_ref, tok_ref, hn_hbm, wg_ref, wu_ref, wd_ref, y_ref,
                  xbuf, wg_s, wu_s, wd_s, sems, *, tb):
    i = pl.program_id(0)
    n_used = nu_ref[0]
    slot = i % 2

    @pl.when(i == 0)
    def _():
        _row_gather_start(hn_hbm, tok_ref, 0, xbuf.at[0], 0, sems.at[0], tb)

    @pl.when(i + 1 < n_used)
    def _():
        _row_gather_start(hn_hbm, tok_ref, (i + 1) * tb, xbuf.at[1 - slot], 0, sems.at[1 - slot], tb)

    new_expert = (i == 0) | (be_ref[i] != be_ref[jnp.maximum(i - 1, 0)])

    @pl.when((i < n_used) & new_expert)
    def _():
        wg_s[...] = wg_ref[0].astype(BF16)
        wu_s[...] = wu_ref[0].astype(BF16)
        wd_s[...] = wd_ref[0].astype(BF16)

    @pl.when(i < n_used)
    def _():
        _row_gather_wait(hn_hbm, xbuf.at[slot], sems.at[slot])
        x = xbuf[slot].astype(BF16)
        hg = jnp.dot(x, wg_s[...], preferred_element_type=F32)
        hu = jnp.dot(x, wu_s[...], preferred_element_type=F32)
        hid = (hg * jax.nn.sigmoid(hg) * hu).astype(BF16)
        y_ref[...] = jnp.dot(hid, wd_s[...], preferred_element_type=F32)

    @pl.when(i >= n_used)
    def _():
        y_ref[...] = jnp.zeros_like(y_ref)


def _experts(blk_expert, n_used, row_tok, hn, wg, wu, wd, *, tb):
    n_blocks = blk_expert.shape[0]
    wmap = lambda i, be, nu, tok: (be[i], 0, 0)
    grid_spec = pltpu.PrefetchScalarGridSpec(
        num_scalar_prefetch=3,
        grid=(n_blocks,),
        in_specs=[pl.BlockSpec(memory_space=pl.ANY),
                  pl.BlockSpec((1, D_MODEL, D_EXPERT), wmap),
                  pl.BlockSpec((1, D_MODEL, D_EXPERT), wmap),
                  pl.BlockSpec((1, D_EXPERT, D_MODEL), wmap)],
        out_specs=pl.BlockSpec((tb, D_MODEL), lambda i, be, nu, tok: (i, 0)),
        scratch_shapes=[pltpu.VMEM((2, tb, D_MODEL), F32),
                        pltpu.VMEM((D_MODEL, D_EXPERT), BF16), pltpu.VMEM((D_MODEL, D_EXPERT), BF16),
                        pltpu.VMEM((D_EXPERT, D_MODEL), BF16), pltpu.SemaphoreType.DMA((2,))],
    )
    return pl.pallas_call(
        functools.partial(_experts_body, tb=tb),
        grid_spec=grid_spec,
        out_shape=_sds((n_blocks * tb, D_MODEL), F32),
        compiler_params=_cparams(("arbitrary",)),
        name="moe_experts",
    )(blk_expert, n_used, row_tok, hn, wg, wu, wd)


def _combine_body(dest_ref, x_ref, iw_ref, y_hbm, o_ref, ybuf, sems, *, tm, n):
    i = pl.program_id(0)
    n_steps = pl.num_programs(0)
    slot = i % 2

    def start(step, to_slot):
        for k in range(2):
            _row_gather_start(y_hbm, dest_ref, k * n + step * tm, ybuf.at[to_slot], k * tm,
                              sems.at[to_slot], tm)

    @pl.when(i == 0)
    def _():
        start(0, 0)

    @pl.when(i + 1 < n_steps)
    def _():
        start(i + 1, 1 - slot)

    _row_gather_wait(y_hbm, ybuf.at[slot], sems.at[slot])
    iw = iw_ref[...]
    o_ref[...] = (x_ref[...] + iw[:, 0:1] * ybuf[slot, 0:tm, :] + iw[:, 1:2] * ybuf[slot, tm:2 * tm, :])


def _combine(dest_flat, x1, iw, y, *, tm):
    n = x1.shape[0]
    grid_spec = pltpu.PrefetchScalarGridSpec(
        num_scalar_prefetch=1,
        grid=(n // tm,),
        in_specs=[pl.BlockSpec((tm, D_MODEL), lambda i, d: (i, 0)),
                  pl.BlockSpec((tm, LANES), lambda i, d: (i, 0)),
                  pl.BlockSpec(memory_space=pl.ANY)],
        out_specs=pl.BlockSpec((tm, D_MODEL), lambda i, d: (i, 0)),
        scratch_shapes=[pltpu.VMEM((2, 2 * tm, D_MODEL), F32), pltpu.SemaphoreType.DMA((2,))],
    )
    return pl.pallas_call(
        functools.partial(_combine_body, tm=tm, n=n),
        grid_spec=grid_spec,
        out_shape=_sds((n, D_MODEL), F32),
        compiler_params=_cparams(("arbitrary",)),
        name="moe_combine",
    )(dest_flat, x1, iw, y)


def _moe(x1, norm_g, wr2, wg, wu, wd, *, tm, tb):
    n = x1.shape[0]
    hn, ii, iw, cnt = _router(x1, norm_g, wr2, tm=tm)
    counts = cnt[0, :N_EXPERTS]
    padded = (counts + tb - 1) // tb * tb
    pend = jnp.cumsum(padded)
    pstart = pend - padded
    n_blocks = -(-2 * n // tb) + N_EXPERTS
    eids = jnp.arange(N_EXPERTS, dtype=I32)
    e12 = ii[:, 0:2]
    start12 = jnp.sum(jnp.where(e12[:, :, None] == eids, pstart, 0), axis=-1)
    dest_flat = (start12 + ii[:, 2:4]).T.reshape(-1).astype(I32)
    blk_start = jnp.arange(n_blocks, dtype=I32) * tb
    blk_expert = jnp.minimum(jnp.sum(blk_start[:, None] >= pend[None, :], axis=-1), N_EXPERTS - 1).astype(I32)
    n_used = (pend[-1:] // tb).astype(I32)
    row_tok = _row_token(dest_flat, n_rows=n_blocks * tb)
    y = _experts(blk_expert, n_used, row_tok, hn, wg, wu, wd, tb=tb)
    return _combine(dest_flat, x1, iw, y, tm=tm)


def _rope_tables(pos):
    half = ROT_DIM // 2
    inv = jnp.power(ROPE_THETA, -jnp.arange(0, ROT_DIM, 2, dtype=F32) / ROT_DIM)
    ang = pos.astype(F32)[:, None] * inv[None, :]
    cos, sin = jnp.cos(ang), jnp.sin(ang)
    t = pos.shape[0]
    rest = HEAD_DIM - ROT_DIM
    cos_t = jnp.concatenate([cos, cos, jnp.ones((t, rest), F32)], axis=1)
    sa = jnp.concatenate([jnp.zeros((t, half), F32), sin, jnp.zeros((t, rest), F32)], axis=1)
    sb = jnp.concatenate([-sin, jnp.zeros((t, half + rest), F32)], axis=1)
    return cos_t, sa, sb


def _head_match_mask(rows, cols):
    r = np.arange(rows)[:, None]
    c = np.arange(cols)[None, :]
    return np.where((c % N_KV) == ((r % N_HEADS) // GROUP), 0.0, NEG).astype(np.float32)


def _prep_weights(w_in, b_forget, gains, w_branch_fox, w_branch_moba, w_out, w_router_group,
                  w_router_expert):
    fq, fk, mq, mk = gains
    off_f = Q_W + 2 * KV_W
    w_main = jnp.concatenate([w_in[:, :off_f], w_in[:, off_f + N_HEADS:]], axis=1).astype(BF16)
    w_f = jnp.pad(w_in[:, off_f:off_f + N_HEADS], ((0, 0), (0, LANES - N_HEADS))).astype(BF16)
    b_f = jnp.pad(b_forget, (0, LANES - N_HEADS))[None, :]
    ones = lambda w: jnp.ones((w,), F32)
    colgain = jnp.concatenate([
        jnp.tile(fq, N_HEADS), jnp.tile(fk, N_KV), ones(KV_W),
        jnp.tile(mq, N_HEADS), jnp.tile(mk, N_KV), ones(KV_W), ones(2 * D_MODEL)])[None, :]
    wr = jnp.pad(jnp.concatenate([w_router_group, w_router_expert], axis=1),
                 ((0, 0), (0, LANES - N_GROUPS - N_EXPERTS)))
    wr_hi = wr.astype(BF16)
    wr_lo = (wr - wr_hi.astype(F32)).astype(BF16)
    return dict(w_main=w_main, w_f=w_f, b_f=b_f, colgain=colgain,
                wa=w_branch_fox.astype(BF16), wb=w_branch_moba.astype(BF16), wo=w_out.astype(BF16),
                wr2=jnp.concatenate([wr_hi, wr_lo], axis=1))


def _pick(n, pref):
    t = min(n, pref)
    while n % t:
        t //= 2
    return t


def _prompt_layer(x, wts, experts, norm_mix, norm_ffn):
    b, t, _ = x.shape
    n = b * t
    x2 = x.reshape(n, D_MODEL)
    tm = _pick(t, 1024)
    tabs = _rope_tables(jnp.arange(t, dtype=I32))
    proj, logf = _inproj(x2, norm_mix[None, :], wts["w_main"], wts["w_f"], wts["b_f"], wts["colgain"],
                         tabs, tm=tm)
    proj3 = proj.reshape(b, t, PROJ_W)
    c3 = _cumsum(logf.reshape(b, t, LANES), tc=_pick(t, 256))
    ck_t = jnp.transpose(c3[:, :, :N_HEADS], (0, 2, 1))
    oa = _fox_prompt(proj3, c3, ck_t, tq=_pick(t, 512))
    ob = _moba_prompt(proj3)
    x1 = _outproj(oa.reshape(n, Q_W), ob.reshape(n, Q_W), proj, x2, wts["wa"], wts["wb"], wts["wo"],
                  tm=_pick(n, 256))
    y = _moe(x1, norm_ffn[None, :], wts["wr2"], *experts, tm=_pick(n, 256), tb=256)
    return y.reshape(b, t, D_MODEL), proj3, logf.reshape(b, t, LANES)[:, :, :N_HEADS]


def _decode_layer(x, wts, experts, norm_mix, norm_ffn, caches, page_table):
    b, t, _ = x.shape
    n = b * t
    n_pages = page_table.shape[1]
    past = n_pages * PAGE
    assert past % MOBA_BLOCK == 0 and t <= MOBA_BLOCK and (t * N_HEADS) % SUBLANES == 0
    cache_fk, cache_fv, cache_logf, cache_mk, cache_mv = caches
    n_pool = cache_fk.shape[0]
    x2 = x.reshape(n, D_MODEL)
    pos = past + jnp.arange(t, dtype=I32)
    tabs = tuple(jnp.tile(tb_, (b, 1)) for tb_ in _rope_tables(pos))
    proj, logf = _inproj(x2, norm_mix[None, :], wts["w_main"], wts["w_f"], wts["b_f"], wts["colgain"],
                         tabs, tm=n)
    pt_flat = page_table.reshape(-1).astype(I32)
    rows = t * N_HEADS
    logf_bt = logf.reshape(b, t, LANES)[:, :, :N_HEADS]
    logf_new_t = jnp.pad(jnp.transpose(logf_bt, (0, 2, 1)), ((0, 0), (0, 0), (0, LANES - t)))
    nck_rep, c_new_t = _paged_cumsum(pt_flat, jnp.transpose(cache_logf, (0, 2, 1)), logf_new_t,
                                     n_pages=n_pages, pages=_pick(n_pages, 16))
    c_new_t = c_new_t[:, :, :t]
    cq32 = jnp.transpose(c_new_t, (0, 2, 1)).reshape(b, rows, 1)
    r_t = np.arange(rows)[:, None] // N_HEADS
    c_t = np.arange(t * N_KV)[None, :] // N_KV
    new_mask = _head_match_mask(rows, t * N_KV) + np.where(c_t <= r_t, 0.0, NEG).astype(np.float32)
    bias_new = jnp.tile(jnp.repeat(-c_new_t, N_KV, axis=-1), (1, t, 1)) + new_mask[None]
    page_mask = jnp.asarray(_head_match_mask(rows, PAGE_ROWS))
    blk_mask = jnp.asarray(_head_match_mask(rows, MOBA_BLOCK * N_KV))

    def q_rows(col):
        return proj[:, col:col + Q_W].reshape(b, rows, HEAD_DIM)

    def kv_rows(col):
        return proj[:, col:col + KV_W].reshape(b, t * N_KV, HEAD_DIM)

    page_rows = lambda c: c.reshape(n_pool, PAGE_ROWS, HEAD_DIM)
    oa = _fox_decode(pt_flat, q_rows(COL_FOX_Q), cq32, nck_rep, page_mask, kv_rows(COL_FOX_K),
                     kv_rows(COL_FOX_V), bias_new, page_rows(cache_fk), page_rows(cache_fv),
                     n_pages=n_pages, pages=_pick(n_pages, 8))
    ob = _moba_decode(pt_flat, q_rows(COL_MOBA_Q), blk_mask, kv_rows(COL_MOBA_K), kv_rows(COL_MOBA_V),
                      jnp.asarray(new_mask), page_rows(cache_mk), page_rows(cache_mv),
                      n_pages=n_pages, nb=_pick(n_pages * PAGE // MOBA_BLOCK, 4))
    x1 = _outproj(oa.reshape(n, Q_W), ob.reshape(n, Q_W), proj, x2, wts["wa"], wts["wb"], wts["wo"], tm=n)
    y = _moe(x1, norm_ffn[None, :], wts["wr2"], *experts, tm=n, tb=16)
    return y.reshape(b, t, D_MODEL), proj.reshape(b, t, PROJ_W), logf_bt


def _kv_outputs(proj3, logf):
    b, t, _ = proj3.shape
    kv = lambda col: proj3[:, :, col:col + KV_W].reshape(1, b, t, N_KV, HEAD_DIM)
    return kv(COL_FOX_K), kv(COL_FOX_V), logf[None], kv(COL_MOBA_K), kv(COL_MOBA_V)


def kernel(x_prompt, x_sample, cache_fox_k, cache_fox_v, cache_fox_logf, cache_moba_k, cache_moba_v,
           page_table, norm_mix, w_in, b_forget, fox_q_norm, fox_k_norm, moba_q_norm, moba_k_norm,
           w_branch_fox, w_branch_moba, w_out, norm_ffn, w_router_group, w_router_expert,
           w_exp_gate, w_exp_up, w_exp_down):
    assert w_in.shape[0] == 1, "single-layer trunk"
    wts = _prep_weights(w_in[0], b_forget[0], (fox_q_norm[0], fox_k_norm[0], moba_q_norm[0], moba_k_norm[0]),
                        w_branch_fox[0], w_branch_moba[0], w_out[0], w_router_group[0], w_router_expert[0])
    experts = (w_exp_gate[0], w_exp_up[0], w_exp_down[0])
    caches = (cache_fox_k[0], cache_fox_v[0], cache_fox_logf[0], cache_moba_k[0], cache_moba_v[0])
    ys, proj_s, logf_s = _decode_layer(x_sample, wts, experts, norm_mix[0], norm_ffn[0], caches, page_table)
    yp, proj_p, logf_p = _prompt_layer(x_prompt, wts, experts, norm_mix[0], norm_ffn[0])
    return (yp, ys) + _kv_outputs(proj_p, logf_p) + _kv_outputs(proj_s, logf_s)
```

```python
import functools

import numpy as np
import jax
import jax.numpy as jnp
from jax import lax
from jax.experimental import pallas as pl
from jax.experimental.pallas import tpu as pltpu

F32 = jnp.float32
BF16 = jnp.bfloat16
I32 = jnp.int32
HIGHEST = lax.Precision.HIGHEST

LANES = 128
SUBLANES = 8
D_MODEL = 2048
HEAD_DIM = 128
N_HEADS = 8
N_KV = 4
GROUP = N_HEADS // N_KV
ROT_DIM = HEAD_DIM // 4
ROPE_THETA = 500000.0
ATTN_SCALE = HEAD_DIM ** -0.5
MOBA_BLOCK = 256
MOBA_TOPK = 3
PAGE = 128
PAGE_ROWS = PAGE * N_KV
N_GROUPS = 4
EXPERTS_PER_GROUP = 8
N_EXPERTS = N_GROUPS * EXPERTS_PER_GROUP
D_EXPERT = 512
RMS_EPS = 1e-6
NEG = -1e30
Q_W = N_HEADS * HEAD_DIM
KV_W = N_KV * HEAD_DIM
PROJ_W = 2 * (Q_W + 2 * KV_W) + 2 * D_MODEL
PROJ_TILE = 512
COL_FOX_Q, COL_FOX_K, COL_FOX_V = 0, Q_W, Q_W + KV_W
COL_MOBA_Q = Q_W + 2 * KV_W
COL_MOBA_K, COL_MOBA_V = COL_MOBA_Q + Q_W, COL_MOBA_Q + Q_W + KV_W
COL_GA = 2 * (Q_W + 2 * KV_W)
COL_GB = COL_GA + D_MODEL
VMEM_LIMIT = 56 * 1024 * 1024


def _cparams(sem):
    return pltpu.CompilerParams(dimension_semantics=sem, vmem_limit_bytes=VMEM_LIMIT)


def _sds(shape, dtype):
    return jax.ShapeDtypeStruct(shape, dtype)


def _head_rmsnorm(y, gain):
    ms = jnp.mean(y * y, axis=-1, keepdims=True)
    return y * lax.rsqrt(ms + RMS_EPS) * gain


def _inproj_body(x_ref, g_ref, w_ref, wf_ref, bf_ref, cg_ref, cos_ref, sa_ref, sb_ref,
                 proj_ref, logf_ref, h_scr):
    j = pl.program_id(1)

    @pl.when(j == 0)
    def _():
        x = x_ref[...]
        ms = jnp.mean(x * x, axis=-1, keepdims=True)
        h = (x * lax.rsqrt(ms + RMS_EPS) * g_ref[...]).astype(BF16)
        h_scr[...] = h
        z = jnp.dot(h, wf_ref[...], preferred_element_type=F32) + bf_ref[...]
        logf_ref[...] = -(jnp.maximum(-z, 0.0) + jnp.log1p(jnp.exp(-jnp.abs(z))))

    fox_normed = (j >= COL_FOX_Q // PROJ_TILE) & (j < COL_FOX_V // PROJ_TILE)
    moba_normed = (j >= COL_MOBA_Q // PROJ_TILE) & (j < COL_MOBA_V // PROJ_TILE)

    pair_w = 2 * HEAD_DIM

    def pair_heads(pp):
        acc = jnp.dot(h_scr[...], w_ref[:, pp * pair_w:(pp + 1) * pair_w], preferred_element_type=F32)
        for hh in range(2):
            sl = slice(pp * pair_w + hh * HEAD_DIM, pp * pair_w + (hh + 1) * HEAD_DIM)
            yield sl, _head_rmsnorm(acc[:, hh * HEAD_DIM:(hh + 1) * HEAD_DIM], cg_ref[:, sl])

    @pl.when(jnp.logical_not(fox_normed | moba_normed))
    def _():
        proj_ref[...] = jnp.dot(h_scr[...], w_ref[...], preferred_element_type=F32)

    @pl.when(fox_normed)
    def _():
        for pp in range(PROJ_TILE // pair_w):
            for sl, y in pair_heads(pp):
                proj_ref[:, sl] = y

    @pl.when(moba_normed)
    def _():
        cos, sa, sb = cos_ref[...], sa_ref[...], sb_ref[...]
        half = ROT_DIM // 2
        acc = jnp.dot(h_scr[...], w_ref[...], preferred_element_type=F32)
        for hh in range(PROJ_TILE // HEAD_DIM):
            sl = slice(hh * HEAD_DIM, (hh + 1) * HEAD_DIM)
            y = _head_rmsnorm(acc[:, sl], cg_ref[:, sl])
            proj_ref[:, sl] = (y * cos + pltpu.roll(y, half, axis=1) * sa
                               + pltpu.roll(y, HEAD_DIM - half, axis=1) * sb)


def _inproj(x2, g, w_main, w_f, b_f, colgain, tabs, *, tm):
    n = x2.shape[0]
    ntab = tabs[0].shape[0] // tm
    tab_spec = pl.BlockSpec((tm, LANES), lambda i, j: (i % ntab, 0))
    return pl.pallas_call(
        _inproj_body,
        grid=(n // tm, PROJ_W // PROJ_TILE),
        in_specs=[
            pl.BlockSpec((tm, D_MODEL), lambda i, j: (i, 0)),
            pl.BlockSpec((1, D_MODEL), lambda i, j: (0, 0)),
            pl.BlockSpec((D_MODEL, PROJ_TILE), lambda i, j: (0, j)),
            pl.BlockSpec((D_MODEL, LANES), lambda i, j: (0, 0)),
            pl.BlockSpec((1, LANES), lambda i, j: (0, 0)),
            pl.BlockSpec((1, PROJ_TILE), lambda i, j: (0, j)),
            tab_spec, tab_spec, tab_spec,
        ],
        out_specs=[pl.BlockSpec((tm, PROJ_TILE), lambda i, j: (i, j)),
                   pl.BlockSpec((tm, LANES), lambda i, j: (i, 0))],
        out_shape=[_sds((n, PROJ_W), F32), _sds((n, LANES), F32)],
        scratch_shapes=[pltpu.VMEM((tm, D_MODEL), BF16)],
        compiler_params=_cparams(("parallel", "arbitrary")),
        name="inproj",
    )(x2, g, w_main, w_f, b_f, colgain, *tabs)


def _lower_tri(n, strict=False):
    r = lax.broadcasted_iota(I32, (n, n), 0)
    c = lax.broadcasted_iota(I32, (n, n), 1)
    return (c < r) if strict else (c <= r)


def _cumsum_body(x_ref, o_ref, carry):
    @pl.when(pl.program_id(1) == 0)
    def _():
        carry[...] = jnp.zeros_like(carry)

    x = x_ref[0]
    tc = x.shape[0]
    tri = _lower_tri(tc).astype(F32)
    cs = jnp.dot(tri, x, precision=HIGHEST, preferred_element_type=F32) + carry[...]
    o_ref[0] = cs
    carry[...] = cs[tc - 1:tc, :]


def _cumsum(logf3, *, tc):
    b, t, w = logf3.shape
    return pl.pallas_call(
        _cumsum_body,
        grid=(b, t // tc),
        in_specs=[pl.BlockSpec((1, tc, w), lambda i, j: (i, j, 0))],
        out_specs=pl.BlockSpec((1, tc, w), lambda i, j: (i, j, 0)),
        out_shape=_sds((b, t, w), F32),
        scratch_shapes=[pltpu.VMEM((1, w), F32)],
        compiler_params=_cparams(("parallel", "arbitrary")),
        name="logf_cumsum",
    )(logf3)


def _dot_exact01(x, m01):
    hi = x.astype(BF16)
    r1 = x - hi.astype(F32)
    mid = r1.astype(BF16)
    lo = (r1 - mid.astype(F32)).astype(BF16)
    return (jnp.dot(hi, m01, preferred_element_type=F32) + jnp.dot(mid, m01, preferred_element_type=F32)
            + jnp.dot(lo, m01, preferred_element_type=F32))


def _paged_cumsum_body(pt_ref, *refs, pages):
    del pt_ref
    page_refs = refs[:pages]
    new_ref, cum_rep_ref, cum_ref = refs[pages:pages + 3]
    o_ref, onew_ref, carry = refs[pages + 3:]
    s = pl.program_id(1)

    @pl.when(s == 0)
    def _():
        carry[...] = jnp.zeros_like(carry)

    x = jnp.concatenate([r[0] for r in page_refs], axis=0)
    w = _dot_exact01(x, cum_rep_ref[...]).reshape(pages, N_HEADS, PAGE_ROWS)
    tot = w[:, :, PAGE_ROWS - 1:PAGE_ROWS]
    inc = tot
    sh = 1
    while sh < pages:
        inc = inc + jnp.concatenate([jnp.zeros((sh, N_HEADS, 1), F32), inc[:pages - sh]], axis=0)
        sh *= 2
    offs = jnp.concatenate([jnp.zeros((1, N_HEADS, 1), F32), inc[:pages - 1]], axis=0) + carry[...]
    for p in range(pages):
        o_ref[0, :, p * PAGE_ROWS:(p + 1) * PAGE_ROWS] = -(w[p] + offs[p])
    c = carry[...] + inc[pages - 1]
    carry[...] = c

    @pl.when(s == pl.num_programs(1) - 1)
    def _():
        onew_ref[0] = _dot_exact01(new_ref[0], cum_ref[...]) + c


def _paged_cumsum(pt_flat, cache_logf_t, logf_new_t, *, n_pages, pages):
    b = logf_new_t.shape[0]
    upper = np.triu(np.ones((PAGE, PAGE), np.float32))
    cum = jnp.asarray(upper, BF16)
    cum_rep = jnp.asarray(np.repeat(upper, N_KV, axis=1), BF16)

    def page_map(p):
        return lambda i, s, pt: (pt[i * n_pages + s * pages + p], 0, 0)

    const = lambda i, s, pt: (0, 0)
    grid_spec = pltpu.PrefetchScalarGridSpec(
        num_scalar_prefetch=1,
        grid=(b, n_pages // pages),
        in_specs=[pl.BlockSpec((1, N_HEADS, PAGE), page_map(p)) for p in range(pages)]
        + [pl.BlockSpec((1, N_HEADS, LANES), lambda i, s, pt: (i, 0, 0)),
           pl.BlockSpec((PAGE, PAGE_ROWS), const), pl.BlockSpec((PAGE, PAGE), const)],
        out_specs=[pl.BlockSpec((1, N_HEADS, pages * PAGE_ROWS), lambda i, s, pt: (i, 0, s)),
                   pl.BlockSpec((1, N_HEADS, LANES), lambda i, s, pt: (i, 0, 0))],
        scratch_shapes=[pltpu.VMEM((N_HEADS, 1), F32)],
    )
    return pl.pallas_call(
        functools.partial(_paged_cumsum_body, pages=pages),
        grid_spec=grid_spec,
        out_shape=[_sds((b, N_HEADS, n_pages * PAGE_ROWS), F32), _sds((b, N_HEADS, LANES), F32)],
        compiler_params=_cparams(("parallel", "arbitrary")),
        name="paged_logf_cumsum",
    )(pt_flat, *([cache_logf_t] * pages), logf_new_t, cum_rep, cum)


def _softmax_tile(t, rowshift, v, carry):
    m, l, acc = carry
    m_new = jnp.maximum(m, jnp.max(t, axis=-1, keepdims=True) + rowshift)
    p = jnp.exp(t + (rowshift - m_new))
    alpha = jnp.exp(m - m_new)
    l = alpha * l + jnp.sum(p, axis=-1, keepdims=True)
    acc = alpha * acc + jnp.dot(p.astype(BF16), v, preferred_element_type=F32)
    return m_new, l, acc


def _softmax_init(rows):
    return (jnp.full((rows, 1), NEG, F32), jnp.zeros((rows, 1), F32), jnp.zeros((rows, HEAD_DIM), F32))


def _qk(q, k):
    return lax.dot_general(q, k, (((1,), (1,)), ((), ())), preferred_element_type=F32)


def _fox_body(q_ref, k_ref, v_ref, cq_ref, ck_ref, o_ref, *, tq):
    h = pl.program_id(1)
    qi = pl.program_id(2)
    lane = lax.broadcasted_iota(I32, (tq, LANES), 1)
    qs, cqs = [], []
    for g in range(GROUP):
        qs.append((q_ref[0, :, g * HEAD_DIM:(g + 1) * HEAD_DIM] * ATTN_SCALE).astype(BF16))
        cqs.append(jnp.sum(jnp.where(lane == h * GROUP + g, cq_ref[0], 0.0), axis=-1, keepdims=True))

    def tile(start, diag, carries):
        k = k_ref[0, pl.ds(start, tq), :].astype(BF16)
        v = v_ref[0, pl.ds(start, tq), :].astype(BF16)
        out = []
        for g in range(GROUP):
            ck = ck_ref[0, pl.ds(h * GROUP + g, 1), pl.ds(start, tq)]
            t = _qk(qs[g], k) - ck
            if diag:
                t = jnp.where(_lower_tri(tq), t, NEG)
            out.append(_softmax_tile(t, cqs[g], v, carries[g]))
        return tuple(out)

    carries = tile(pl.multiple_of(qi * tq, tq), True, (_softmax_init(tq),) * GROUP)
    carries = lax.fori_loop(0, qi, lambda j, c: tile(pl.multiple_of(j * tq, tq), False, c), carries)
    for g in range(GROUP):
        _, l, acc = carries[g]
        o_ref[0, :, g * HEAD_DIM:(g + 1) * HEAD_DIM] = (acc / l).astype(o_ref.dtype)


def _fox_prompt(proj3, c3, ck_t, *, tq):
    b, t, _ = proj3.shape
    gw = GROUP * HEAD_DIM
    qb, kb, vb = COL_FOX_Q // gw, COL_FOX_K // HEAD_DIM, COL_FOX_V // HEAD_DIM
    return pl.pallas_call(
        functools.partial(_fox_body, tq=tq),
        grid=(b, N_KV, t // tq),
        in_specs=[
            pl.BlockSpec((1, tq, gw), lambda i, h, q: (i, q, qb + h)),
            pl.BlockSpec((1, t, HEAD_DIM), lambda i, h, q: (i, 0, kb + h)),
            pl.BlockSpec((1, t, HEAD_DIM), lambda i, h, q: (i, 0, vb + h)),
            pl.BlockSpec((1, tq, LANES), lambda i, h, q: (i, q, 0)),
            pl.BlockSpec((1, N_HEADS, t), lambda i, h, q: (i, 0, 0)),
        ],
        out_specs=pl.BlockSpec((1, tq, gw), lambda i, h, q: (i, q, h)),
        out_shape=_sds((b, t, Q_W), BF16),
        compiler_params=_cparams(("parallel", "parallel", "arbitrary")),
        name="fox_prompt",
    )(proj3, proj3, proj3, c3, ck_t)


def _topk_select(g, elig, n_cand, axis):
    idx = lax.broadcasted_iota(I32, g.shape, axis)
    g = jnp.where(elig, g, -jnp.inf)
    cnt = jnp.zeros(g.shape, F32)
    for m in range(n_cand):
        gm = g[m:m + 1, :] if axis == 0 else g[:, m:m + 1]
        beats = (gm > g) | ((gm == g) & (m < idx))
        cnt = cnt + jnp.where(beats, 1.0, 0.0)
    return elig & (cnt < MOBA_TOPK)


def _moba_body(q_ref, k_ref, v_ref, o_ref, kmean_scr, *, n_blk):
    qi = pl.program_id(2)
    tq = MOBA_BLOCK
    nb8 = kmean_scr.shape[0]

    @pl.when(qi == 0)
    def _():
        kmean_scr[...] = jnp.zeros_like(kmean_scr)
        for n in range(n_blk):
            kb = k_ref[0, n * tq:(n + 1) * tq, :]
            kmean_scr[n:n + 1, :] = jnp.mean(kb, axis=0, keepdims=True)

    eye = jnp.where(lax.broadcasted_iota(I32, (tq, tq), 0) == lax.broadcasted_iota(I32, (tq, tq), 1),
                    1.0, 0.0).astype(BF16)
    blk = lax.broadcasted_iota(I32, (nb8, tq), 0)
    lane_blk = lax.broadcasted_iota(I32, (tq, nb8), 1)
    qs, sels = [], []
    for g in range(GROUP):
        qf = q_ref[0, :, g * HEAD_DIM:(g + 1) * HEAD_DIM]
        qs.append((qf * ATTN_SCALE).astype(BF16))
        gate_t = lax.dot_general(kmean_scr[...], qf, (((1,), (1,)), ((), ())),
                                 precision=HIGHEST, preferred_element_type=F32)
        sel_t = jnp.where(_topk_select(gate_t, blk < qi, n_blk, 0), 1.0, 0.0).astype(BF16)
        sels.append(_qk(eye, sel_t))

    def load(start, size):
        return (k_ref[0, pl.ds(start, size), :].astype(BF16), v_ref[0, pl.ds(start, size), :].astype(BF16))

    def row_bias(g, n):
        flag = jnp.sum(jnp.where(lane_blk == n, sels[g], 0.0), axis=-1, keepdims=True)
        return (flag - 1.0) * (-NEG)

    def past(n, carries):
        k, v = load(pl.multiple_of(n * tq, tq), tq)
        return tuple(_softmax_tile(_qk(qs[g], k) + row_bias(g, n), 0.0, v, carries[g])
                     for g in range(GROUP))

    first_half = lax.broadcasted_iota(I32, (tq, 2 * tq), 1) < tq

    def past_pair(j, carries):
        k, v = load(pl.multiple_of(2 * j * tq, 2 * tq), 2 * tq)
        out = []
        for g in range(GROUP):
            bias = jnp.where(first_half, row_bias(g, 2 * j), row_bias(g, 2 * j + 1))
            out.append(_softmax_tile(_qk(qs[g], k) + bias, 0.0, v, carries[g]))
        return tuple(out)

    k, v = load(pl.multiple_of(qi * tq, tq), tq)
    carries = tuple(
        _softmax_tile(jnp.where(_lower_tri(tq), _qk(qs[g], k), NEG), 0.0, v, _softmax_init(tq))
        for g in range(GROUP))
    carries = lax.fori_loop(0, lax.shift_right_logical(qi, 1), past_pair, carries)
    carries = lax.cond((qi & 1) == 1, lambda c: past(qi - 1, c), lambda c: c, carries)
    for g in range(GROUP):
        _, l, acc = carries[g]
        o_ref[0, :, g * HEAD_DIM:(g + 1) * HEAD_DIM] = (acc / l).astype(o_ref.dtype)


def _moba_prompt(proj3):
    b, t, _ = proj3.shape
    n_blk = t // MOBA_BLOCK
    assert t % MOBA_BLOCK == 0
    nb8 = -(-n_blk // SUBLANES) * SUBLANES
    gw = GROUP * HEAD_DIM
    qb, kb, vb = COL_MOBA_Q // gw, COL_MOBA_K // HEAD_DIM, COL_MOBA_V // HEAD_DIM
    tq = MOBA_BLOCK
    return pl.pallas_call(
        functools.partial(_moba_body, n_blk=n_blk),
        grid=(b, N_KV, t // tq),
        in_specs=[
            pl.BlockSpec((1, tq, gw), lambda i, h, q: (i, q, qb + h)),
            pl.BlockSpec((1, t, HEAD_DIM), lambda i, h, q: (i, 0, kb + h)),
            pl.BlockSpec((1, t, HEAD_DIM), lambda i, h, q: (i, 0, vb + h)),
        ],
        out_specs=pl.BlockSpec((1, tq, gw), lambda i, h, q: (i, q, h)),
        out_shape=_sds((b, t, Q_W), BF16),
        scratch_shapes=[pltpu.VMEM((nb8, HEAD_DIM), F32)],
        compiler_params=_cparams(("parallel", "parallel", "arbitrary")),
        name="moba_prompt",
    )(proj3, proj3, proj3)


def _fox_decode_body(pt_ref, *refs, pages, rows):
    del pt_ref
    q_ref, cq_ref, nck_ref, mask_ref, knew_ref, vnew_ref, bnew_ref = refs[:7]
    k_refs = refs[7:7 + pages]
    v_refs = refs[7 + pages:7 + 2 * pages]
    o_ref, m_scr, l_scr, acc_scr = refs[7 + 2 * pages:]
    s = pl.program_id(1)

    @pl.when(s == 0)
    def _():
        m_scr[...] = jnp.full_like(m_scr, NEG)
        l_scr[...] = jnp.zeros_like(l_scr)
        acc_scr[...] = jnp.zeros_like(acc_scr)

    q = (q_ref[0] * ATTN_SCALE).astype(BF16)
    cq = cq_ref[0]
    m, l, acc = m_scr[...], l_scr[...], acc_scr[...]
    t = jnp.concatenate([_qk(q, k_refs[p][0].astype(BF16)) for p in range(pages)], axis=1)
    t = t + jnp.concatenate([nck_ref[0]] * (rows // N_HEADS), axis=0) + mask_ref[...]
    m_new = jnp.maximum(m, jnp.max(t, axis=-1, keepdims=True) + cq)
    pf = jnp.exp(t + (cq - m_new))
    alpha = jnp.exp(m - m_new)
    l = alpha * l + jnp.sum(pf, axis=-1, keepdims=True)
    p_ = pf.astype(BF16)
    pv = jnp.dot(p_[:, 0:PAGE_ROWS], v_refs[0][0].astype(BF16), preferred_element_type=F32)
    for p in range(1, pages):
        pv = pv + jnp.dot(p_[:, p * PAGE_ROWS:(p + 1) * PAGE_ROWS], v_refs[p][0].astype(BF16),
                          preferred_element_type=F32)
    acc = alpha * acc + pv
    m_scr[...], l_scr[...], acc_scr[...] = m_new, l, acc

    @pl.when(s == pl.num_programs(1) - 1)
    def _():
        tn = _qk(q, knew_ref[0].astype(BF16)) + bnew_ref[0]
        _, l2, acc2 = _softmax_tile(tn, cq, vnew_ref[0].astype(BF16), (m_new, l, acc))
        o_ref[0] = (acc2 / l2).astype(o_ref.dtype)


def _fox_decode(pt_flat, q32, cq32, nck_rep, mask, k_new, v_new, bias_new, cache_k, cache_v,
                *, n_pages, pages):
    b, rows, _ = q32.shape
    n_new = k_new.shape[1]

    def page_map(p):
        return lambda i, s, pt: (pt[i * n_pages + s * pages + p], 0, 0)

    per_b = lambda i, s, pt: (i, 0, 0)
    grid_spec = pltpu.PrefetchScalarGridSpec(
        num_scalar_prefetch=1,
        grid=(b, n_pages // pages),
        in_specs=[
            pl.BlockSpec((1, rows, HEAD_DIM), per_b),
            pl.BlockSpec((1, rows, 1), per_b),
            pl.BlockSpec((1, N_HEADS, pages * PAGE_ROWS), lambda i, s, pt: (i, 0, s)),
            pl.BlockSpec((rows, pages * PAGE_ROWS), lambda i, s, pt: (0, 0)),
            pl.BlockSpec((1, n_new, HEAD_DIM), per_b),
            pl.BlockSpec((1, n_new, HEAD_DIM), per_b),
            pl.BlockSpec((1, rows, n_new), per_b),
        ] + [pl.BlockSpec((1, PAGE_ROWS, HEAD_DIM), page_map(p)) for p in range(pages)] * 2,
        out_specs=pl.BlockSpec((1, rows, HEAD_DIM), per_b),
        scratch_shapes=[pltpu.VMEM((rows, 1), F32), pltpu.VMEM((rows, 1), F32),
                        pltpu.VMEM((rows, HEAD_DIM), F32)],
    )
    return pl.pallas_call(
        functools.partial(_fox_decode_body, pages=pages, rows=rows),
        grid_spec=grid_spec,
        out_shape=_sds((b, rows, HEAD_DIM), BF16),
        compiler_params=_cparams(("parallel", "arbitrary")),
        name="fox_decode",
    )(pt_flat, q32, cq32, nck_rep, jnp.tile(mask, (1, pages)), k_new, v_new, bias_new,
      *([cache_k] * pages), *([cache_v] * pages))


def _tree_sum(x):
    while x.shape[0] > 1:
        half = x.shape[0] // 2
        x = x[:half] + x[half:]
    return x[0]


def _moba_decode_body(pt_ref, *refs, nb, n_past, rows):
    del pt_ref
    ppb = MOBA_BLOCK // PAGE
    pages = nb * ppb
    q_ref, mask_ref, knew_ref, vnew_ref, bnew_ref = refs[:5]
    k_refs = refs[5:5 + pages]
    v_refs = refs[5 + pages:5 + 2 * pages]
    o_ref, m_scr, l_scr, g_scr, acc_scr = refs[5 + 2 * pages:]
    s = pl.program_id(1)

    @pl.when(s == 0)
    def _():
        m_scr[...] = jnp.zeros_like(m_scr)
        l_scr[...] = jnp.zeros_like(l_scr)
        g_scr[...] = jnp.zeros_like(g_scr)

    qf = q_ref[0]
    q = (qf * ATTN_SCALE).astype(BF16)
    lane = lax.broadcasted_iota(I32, (rows, LANES), 1)
    m_all, l_all, g_all = m_scr[...], l_scr[...], g_scr[...]
    for j in range(nb):
        n = s * nb + j
        kf = jnp.concatenate([k_refs[j * ppb + p][0] for p in range(ppb)], axis=0)
        vf = jnp.concatenate([v_refs[j * ppb + p][0] for p in range(ppb)], axis=0)
        t = _qk(q, kf.astype(BF16)) + mask_ref[...]
        m_n = jnp.max(t, axis=-1, keepdims=True)
        p_ = jnp.exp(t - m_n)
        l_n = jnp.sum(p_, axis=-1, keepdims=True)
        acc_scr[n] = jnp.dot(p_.astype(BF16), vf.astype(BF16), preferred_element_type=F32)
        s8 = _tree_sum(kf.reshape(MOBA_BLOCK * N_KV // SUBLANES, SUBLANES, HEAD_DIM))
        ks = s8[0:N_KV] + s8[N_KV:2 * N_KV]
        ks_h = jnp.concatenate([ks[h // GROUP:h // GROUP + 1] for h in range(N_HEADS)], axis=0)
        ks_r = jnp.concatenate([ks_h] * (rows // N_HEADS), axis=0)
        g_n = jnp.sum(qf * ks_r, axis=-1, keepdims=True) * (1.0 / MOBA_BLOCK)
        m_all = jnp.where(lane == n, m_n, m_all)
        l_all = jnp.where(lane == n, l_n, l_all)
        g_all = jnp.where(lane == n, g_n, g_all)
    m_scr[...], l_scr[...], g_scr[...] = m_all, l_all, g_all

    @pl.when(s == pl.num_programs(1) - 1)
    def _():
        t = _qk(q, knew_ref[0].astype(BF16)) + bnew_ref[...]
        m_own = jnp.max(t, axis=-1, keepdims=True)
        p_own = jnp.exp(t - m_own)
        l_own = jnp.sum(p_own, axis=-1, keepdims=True)
        acc_own = jnp.dot(p_own.astype(BF16), vnew_ref[0].astype(BF16), preferred_element_type=F32)
        sel = _topk_select(g_all, lane < n_past, n_past, 1)
        m_x = jnp.maximum(m_own, jnp.max(jnp.where(sel, m_all, NEG), axis=-1, keepdims=True))
        w = jnp.where(sel, jnp.exp(m_all - m_x), 0.0)
        w_own = jnp.exp(m_own - m_x)
        den = w_own * l_own + jnp.sum(w * l_all, axis=-1, keepdims=True)
        num = w_own * acc_own
        for n in range(n_past):
            num = num + w[:, n:n + 1] * acc_scr[n]
        o_ref[0] = (num / den).astype(o_ref.dtype)


def _moba_decode(pt_flat, q32, mask, k_new, v_new, bias_new, cache_k, cache_v, *, n_pages, nb):
    b, rows, _ = q32.shape
    ppb = MOBA_BLOCK // PAGE
    pages = nb * ppb
    n_past = n_pages // ppb
    assert n_past <= LANES and n_past % nb == 0
    n_new = k_new.shape[1]

    def page_map(p):
        return lambda i, s, pt: (pt[i * n_pages + s * pages + p], 0, 0)

    per_b = lambda i, s, pt: (i, 0, 0)
    grid_spec = pltpu.PrefetchScalarGridSpec(
        num_scalar_prefetch=1,
        grid=(b, n_past // nb),
        in_specs=[
            pl.BlockSpec((1, rows, HEAD_DIM), per_b),
            pl.BlockSpec((rows, ppb * PAGE_ROWS), lambda i, s, pt: (0, 0)),
            pl.BlockSpec((1, n_new, HEAD_DIM), per_b),
            pl.BlockSpec((1, n_new, HEAD_DIM), per_b),
            pl.BlockSpec((rows, n_new), lambda i, s, pt: (0, 0)),
        ] + [pl.BlockSpec((1, PAGE_ROWS, HEAD_DIM), page_map(p)) for p in range(pages)] * 2,
        out_specs=pl.BlockSpec((1, rows, HEAD_DIM), per_b),
        scratch_shapes=[pltpu.VMEM((rows, LANES), F32), pltpu.VMEM((rows, LANES), F32),
                        pltpu.VMEM((rows, LANES), F32), pltpu.VMEM((n_past, rows, HEAD_DIM), F32)],
    )
    return pl.pallas_call(
        functools.partial(_moba_decode_body, nb=nb, n_past=n_past, rows=rows),
        grid_spec=grid_spec,
        out_shape=_sds((b, rows, HEAD_DIM), BF16),
        compiler_params=_cparams(("parallel", "arbitrary")),
        name="moba_decode",
    )(pt_flat, q32, mask, k_new, v_new, bias_new, *([cache_k] * pages), *([cache_v] * pages))


def _outproj_body(oa_ref, ob_ref, ga_ref, gb_ref, x_ref, wa_ref, wb_ref, wo_ref, o_ref):
    ya = jnp.dot(oa_ref[...], wa_ref[...], preferred_element_type=F32)
    yb = jnp.dot(ob_ref[...], wb_ref[...], preferred_element_type=F32)
    mixed = jax.nn.sigmoid(ga_ref[...]) * ya + jax.nn.sigmoid(gb_ref[...]) * yb
    o_ref[...] = x_ref[...] + jnp.dot(mixed.astype(BF16), wo_ref[...], preferred_element_type=F32)


def _outproj(oa, ob, proj, x2, wa, wb, wo, *, tm):
    n = x2.shape[0]
    const = lambda i: (0, 0)
    once = pl.Buffered(1)
    return pl.pallas_call(
        _outproj_body,
        grid=(n // tm,),
        in_specs=[
            pl.BlockSpec((tm, Q_W), lambda i: (i, 0)),
            pl.BlockSpec((tm, Q_W), lambda i: (i, 0)),
            pl.BlockSpec((tm, D_MODEL), lambda i: (i, COL_GA // D_MODEL)),
            pl.BlockSpec((tm, D_MODEL), lambda i: (i, COL_GB // D_MODEL)),
            pl.BlockSpec((tm, D_MODEL), lambda i: (i, 0)),
            pl.BlockSpec((Q_W, D_MODEL), const, pipeline_mode=once),
            pl.BlockSpec((Q_W, D_MODEL), const, pipeline_mode=once),
            pl.BlockSpec((D_MODEL, D_MODEL), const, pipeline_mode=once),
        ],
        out_specs=pl.BlockSpec((tm, D_MODEL), lambda i: (i, 0)),
        out_shape=_sds((n, D_MODEL), F32),
        compiler_params=_cparams(("parallel",)),
        name="outproj",
    )(oa, ob, proj, proj, x2, wa, wb, wo)


def _first_lane(hit, lane):
    return jnp.min(jnp.where(hit, lane.astype(F32), float(LANES)), axis=-1, keepdims=True).astype(I32)


def _router_body(x_ref, g_ref, wr_ref, hn_ref, ii_ref, iw_ref, cnt_ref, cnt_scr):
    @pl.when(pl.program_id(0) == 0)
    def _():
        cnt_scr[...] = jnp.zeros_like(cnt_scr)

    x = x_ref[...]
    tm = x.shape[0]
    ms = jnp.mean(x * x, axis=-1, keepdims=True)
    hn = x * lax.rsqrt(ms + RMS_EPS) * g_ref[...]
    hn_ref[...] = hn
    h_hi = hn.astype(BF16)
    h_lo = (hn - h_hi.astype(F32)).astype(BF16)
    quad = jnp.dot(jnp.concatenate([h_hi, h_lo], axis=0), wr_ref[...], preferred_element_type=F32)
    logits = (quad[:tm, :LANES] + quad[:tm, LANES:]) + (quad[tm:, :LANES] + quad[tm:, LANES:])
    lane = lax.broadcasted_iota(I32, (tm, LANES), 1)
    is_g = lane < N_GROUPS
    gl = jnp.where(is_g, logits, -jnp.inf)
    gmax = jnp.max(gl, axis=-1, keepdims=True)
    grp = _first_lane(gl == gmax, lane)
    p_grp = 1.0 / jnp.sum(jnp.where(is_g, jnp.exp(gl - gmax), 0.0), axis=-1, keepdims=True)
    e_lane = lane - N_GROUPS
    shift = EXPERTS_PER_GROUP.bit_length() - 1
    in_grp = (e_lane < N_EXPERTS) & (lax.shift_right_arithmetic(e_lane, shift) == grp)
    el = jnp.where(in_grp, logits, -jnp.inf)
    v1 = jnp.max(el, axis=-1, keepdims=True)
    i1 = _first_lane(el == v1, lane)
    el2 = jnp.where(lane == i1, -jnp.inf, el)
    v2 = jnp.max(el2, axis=-1, keepdims=True)
    i2 = _first_lane(el2 == v2, lane)
    r = jnp.exp(v2 - v1)
    w1 = p_grp / (1.0 + r)
    w2 = p_grp * r / (1.0 + r)
    e1 = i1 - N_GROUPS
    e2 = i2 - N_GROUPS
    oh = jnp.where((lane == e1) | (lane == e2), 1.0, 0.0)
    before = cnt_scr[...] + jnp.dot(_lower_tri(tm, strict=True).astype(BF16), oh.astype(BF16),
                                    preferred_element_type=F32)
    r1 = jnp.sum(jnp.where(lane == e1, before, 0.0), axis=-1, keepdims=True)
    r2 = jnp.sum(jnp.where(lane == e2, before, 0.0), axis=-1, keepdims=True)
    cnt_scr[...] = cnt_scr[...] + jnp.sum(oh, axis=0, keepdims=True)
    cnt_ref[...] = cnt_scr[...].astype(I32)
    ii = jnp.where(lane == 0, e1, jnp.where(lane == 1, e2, jnp.where(
        lane == 2, r1.astype(I32), jnp.where(lane == 3, r2.astype(I32), 0))))
    ii_ref[...] = ii
    iw_ref[...] = jnp.where(lane == 0, w1, jnp.where(lane == 1, w2, 0.0))


def _router(x1, g, wr2, *, tm):
    n = x1.shape[0]
    return pl.pallas_call(
        _router_body,
        grid=(n // tm,),
        in_specs=[pl.BlockSpec((tm, D_MODEL), lambda i: (i, 0)),
                  pl.BlockSpec((1, D_MODEL), lambda i: (0, 0)),
                  pl.BlockSpec((D_MODEL, 2 * LANES), lambda i: (0, 0))],
        out_specs=[pl.BlockSpec((tm, D_MODEL), lambda i: (i, 0)),
                   pl.BlockSpec((tm, LANES), lambda i: (i, 0)),
                   pl.BlockSpec((tm, LANES), lambda i: (i, 0)),
                   pl.BlockSpec((1, LANES), lambda i: (0, 0))],
        out_shape=[_sds((n, D_MODEL), F32), _sds((n, LANES), I32), _sds((n, LANES), F32),
                   _sds((1, LANES), I32)],
        scratch_shapes=[pltpu.VMEM((1, LANES), F32)],
        compiler_params=_cparams(("arbitrary",)),
        name="moe_router",
    )(x1, g, wr2)


def _row_token_body(dest_ref, tok_ref, *, n_tok, n_rows):
    def zero(i, c):
        tok_ref[i] = 0
        return c

    lax.fori_loop(0, n_rows, zero, 0, unroll=8)

    def put(t, c):
        tok_ref[dest_ref[t]] = t
        tok_ref[dest_ref[n_tok + t]] = t
        return c

    lax.fori_loop(0, n_tok, put, 0, unroll=8)


def _row_token(dest_flat, *, n_rows):
    n_tok = dest_flat.shape[0] // 2
    return pl.pallas_call(
        functools.partial(_row_token_body, n_tok=n_tok, n_rows=n_rows),
        in_specs=[pl.BlockSpec(memory_space=pltpu.SMEM)],
        out_specs=pl.BlockSpec(memory_space=pltpu.SMEM),
        out_shape=_sds((n_rows,), I32),
        name="moe_row_token",
    )(dest_flat)


GATHER_UNROLL = 8


def _row_gather_start(src_hbm, idx_ref, base, buf, row0, sem, n):
    def body(r, c):
        pltpu.make_async_copy(src_hbm.at[pl.ds(idx_ref[base + r], 1)], buf.at[pl.ds(row0 + r, 1)],
                              sem).start()
        return c

    lax.fori_loop(0, n, body, 0, unroll=GATHER_UNROLL)


def _row_gather_wait(src_hbm, buf, sem):
    pltpu.make_async_copy(src_hbm.at[pl.ds(0, buf.shape[0])], buf, sem).wait()


def _experts_body(be_ref, nu_ref, tok_ref, hn_hbm, wg_ref, wu_ref, wd_ref, y_ref,
                  xbuf, wg_s, wu_s, wd_s, sems, *, tb):
    i = pl.program_id(0)
    n_used = nu_ref[0]
    slot = i % 2

    @pl.when(i == 0)
    def _():
        _row_gather_start(hn_hbm, tok_ref, 0, xbuf.at[0], 0, sems.at[0], tb)

    @pl.when(i + 1 < n_used)
    def _():
        _row_gather_start(hn_hbm, tok_ref, (i + 1) * tb, xbuf.at[1 - slot], 0, sems.at[1 - slot], tb)

    new_expert = (i == 0) | (be_ref[i] != be_ref[jnp.maximum(i - 1, 0)])

    @pl.when((i < n_used) & new_expert)
    def _():
        wg_s[...] = wg_ref[0].astype(BF16)
        wu_s[...] = wu_ref[0].astype(BF16)
        wd_s[...] = wd_ref[0].astype(BF16)

    @pl.when(i < n_used)
    def _():
        _row_gather_wait(hn_hbm, xbuf.at[slot], sems.at[slot])
        x = xbuf[slot].astype(BF16)
        hg = jnp.dot(x, wg_s[...], preferred_element_type=F32)
        hu = jnp.dot(x, wu_s[...], preferred_element_type=F32)
        hid = (hg * jax.nn.sigmoid(hg) * hu).astype(BF16)
        y_ref[...] = jnp.dot(hid, wd_s[...], preferred_element_type=F32)

    @pl.when(i >= n_used)
    def _():
        y_ref[...] = jnp.zeros_like(y_ref)


def _experts(blk_expert, n_used, row_tok, hn, wg, wu, wd, *, tb):
    n_blocks = blk_expert.shape[0]
    wmap = lambda i, be, nu, tok: (be[i], 0, 0)
    grid_spec = pltpu.PrefetchScalarGridSpec(
        num_scalar_prefetch=3,
        grid=(n_blocks,),
        in_specs=[pl.BlockSpec(memory_space=pl.ANY),
                  pl.BlockSpec((1, D_MODEL, D_EXPERT), wmap),
                  pl.BlockSpec((1, D_MODEL, D_EXPERT), wmap),
                  pl.BlockSpec((1, D_EXPERT, D_MODEL), wmap)],
        out_specs=pl.BlockSpec((tb, D_MODEL), lambda i, be, nu, tok: (i, 0)),
        scratch_shapes=[pltpu.VMEM((2, tb, D_MODEL), F32),
                        pltpu.VMEM((D_MODEL, D_EXPERT), BF16), pltpu.VMEM((D_MODEL, D_EXPERT), BF16),
                        pltpu.VMEM((D_EXPERT, D_MODEL), BF16), pltpu.SemaphoreType.DMA((2,))],
    )
    return pl.pallas_call(
        functools.partial(_experts_body, tb=tb),
        grid_spec=grid_spec,
        out_shape=_sds((n_blocks * tb, D_MODEL), F32),
        compiler_params=_cparams(("arbitrary",)),
        name="moe_experts",
    )(blk_expert, n_used, row_tok, hn, wg, wu, wd)


def _combine_body(dest_ref, x_ref, iw_ref, y_hbm, o_ref, ybuf, sems, *, tm, n):
    i = pl.program_id(0)
    n_steps = pl.num_programs(0)
    slot = i % 2

    def start(step, to_slot):
        for k in range(2):
            _row_gather_start(y_hbm, dest_ref, k * n + step * tm, ybuf.at[to_slot], k * tm,
                              sems.at[to_slot], tm)

    @pl.when(i == 0)
    def _():
        start(0, 0)

    @pl.when(i + 1 < n_steps)
    def _():
        start(i + 1, 1 - slot)

    _row_gather_wait(y_hbm, ybuf.at[slot], sems.at[slot])
    iw = iw_ref[...]
    o_ref[...] = (x_ref[...] + iw[:, 0:1] * ybuf[slot, 0:tm, :] + iw[:, 1:2] * ybuf[slot, tm:2 * tm, :])


def _combine(dest_flat, x1, iw, y, *, tm):
    n = x1.shape[0]
    grid_spec = pltpu.PrefetchScalarGridSpec(
        num_scalar_prefetch=1,
        grid=(n // tm,),
        in_specs=[pl.BlockSpec((tm, D_MODEL), lambda i, d: (i, 0)),
                  pl.BlockSpec((tm, LANES), lambda i, d: (i, 0)),
                  pl.BlockSpec(memory_space=pl.ANY)],
        out_specs=pl.BlockSpec((tm, D_MODEL), lambda i, d: (i, 0)),
        scratch_shapes=[pltpu.VMEM((2, 2 * tm, D_MODEL), F32), pltpu.SemaphoreType.DMA((2,))],
    )
    return pl.pallas_call(
        functools.partial(_combine_body, tm=tm, n=n),
        grid_spec=grid_spec,
        out_shape=_sds((n, D_MODEL), F32),
        compiler_params=_cparams(("arbitrary",)),
        name="moe_combine",
    )(dest_flat, x1, iw, y)


def _moe(x1, norm_g, wr2, wg, wu, wd, *, tm, tb):
    n = x1.shape[0]
    hn, ii, iw, cnt = _router(x1, norm_g, wr2, tm=tm)
    counts = cnt[0, :N_EXPERTS]
    padded = (counts + tb - 1) // tb * tb
    pend = jnp.cumsum(padded)
    pstart = pend - padded
    n_blocks = -(-2 * n // tb) + N_EXPERTS
    eids = jnp.arange(N_EXPERTS, dtype=I32)
    e12 = ii[:, 0:2]
    start12 = jnp.sum(jnp.where(e12[:, :, None] == eids, pstart, 0), axis=-1)
    dest_flat = (start12 + ii[:, 2:4]).T.reshape(-1).astype(I32)
    blk_start = jnp.arange(n_blocks, dtype=I32) * tb
    blk_expert = jnp.minimum(jnp.sum(blk_start[:, None] >= pend[None, :], axis=-1), N_EXPERTS - 1).astype(I32)
    n_used = (pend[-1:] // tb).astype(I32)
    row_tok = _row_token(dest_flat, n_rows=n_blocks * tb)
    y = _experts(blk_expert, n_used, row_tok, hn, wg, wu, wd, tb=tb)
    return _combine(dest_flat, x1, iw, y, tm=tm)


def _rope_tables(pos):
    half = ROT_DIM // 2
    inv = jnp.power(ROPE_THETA, -jnp.arange(0, ROT_DIM, 2, dtype=F32) / ROT_DIM)
    ang = pos.astype(F32)[:, None] * inv[None, :]
    cos, sin = jnp.cos(ang), jnp.sin(ang)
    t = pos.shape[0]
    rest = HEAD_DIM - ROT_DIM
    cos_t = jnp.concatenate([cos, cos, jnp.ones((t, rest), F32)], axis=1)
    sa = jnp.concatenate([jnp.zeros((t, half), F32), sin, jnp.zeros((t, rest), F32)], axis=1)
    sb = jnp.concatenate([-sin, jnp.zeros((t, half + rest), F32)], axis=1)
    return cos_t, sa, sb


def _head_match_mask(rows, cols):
    r = np.arange(rows)[:, None]
    c = np.arange(cols)[None, :]
    return np.where((c % N_KV) == ((r % N_HEADS) // GROUP), 0.0, NEG).astype(np.float32)


def _prep_weights(w_in, b_forget, gains, w_branch_fox, w_branch_moba, w_out, w_router_group,
                  w_router_expert):
    fq, fk, mq, mk = gains
    off_f = Q_W + 2 * KV_W
    w_main = jnp.concatenate([w_in[:, :off_f], w_in[:, off_f + N_HEADS:]], axis=1).astype(BF16)
    w_f = jnp.pad(w_in[:, off_f:off_f + N_HEADS], ((0, 0), (0, LANES - N_HEADS))).astype(BF16)
    b_f = jnp.pad(b_forget, (0, LANES - N_HEADS))[None, :]
    ones = lambda w: jnp.ones((w,), F32)
    colgain = jnp.concatenate([
        jnp.tile(fq, N_HEADS), jnp.tile(fk, N_KV), ones(KV_W),
        jnp.tile(mq, N_HEADS), jnp.tile(mk, N_KV), ones(KV_W), ones(2 * D_MODEL)])[None, :]
    wr = jnp.pad(jnp.concatenate([w_router_group, w_router_expert], axis=1),
                 ((0, 0), (0, LANES - N_GROUPS - N_EXPERTS)))
    wr_hi = wr.astype(BF16)
    wr_lo = (wr - wr_hi.astype(F32)).astype(BF16)
    return dict(w_main=w_main, w_f=w_f, b_f=b_f, colgain=colgain,
                wa=w_branch_fox.astype(BF16), wb=w_branch_moba.astype(BF16), wo=w_out.astype(BF16),
                wr2=jnp.concatenate([wr_hi, wr_lo], axis=1))


def _pick(n, pref):
    t = min(n, pref)
    while n % t:
        t //= 2
    return t


def _prompt_layer(x, wts, experts, norm_mix, norm_ffn):
    b, t, _ = x.shape
    n = b * t
    x2 = x.reshape(n, D_MODEL)
    tm = _pick(t, 1024)
    tabs = _rope_tables(jnp.arange(t, dtype=I32))
    proj, logf = _inproj(x2, norm_mix[None, :], wts["w_main"], wts["w_f"], wts["b_f"], wts["colgain"],
                         tabs, tm=tm)
    proj3 = proj.reshape(b, t, PROJ_W)
    c3 = _cumsum(logf.reshape(b, t, LANES), tc=_pick(t, 256))
    ck_t = jnp.transpose(c3[:, :, :N_HEADS], (0, 2, 1))
    oa = _fox_prompt(proj3, c3, ck_t, tq=_pick(t, 512))
    ob = _moba_prompt(proj3)
    x1 = _outproj(oa.reshape(n, Q_W), ob.reshape(n, Q_W), proj, x2, wts["wa"], wts["wb"], wts["wo"],
                  tm=_pick(n, 256))
    y = _moe(x1, norm_ffn[None, :], wts["wr2"], *experts, tm=_pick(n, 256), tb=256)
    return y.reshape(b, t, D_MODEL), proj3, logf.reshape(b, t, LANES)[:, :, :N_HEADS]


def _decode_layer(x, wts, experts, norm_mix, norm_ffn, caches, page_table):
    b, t, _ = x.shape
    n = b * t
    n_pages = page_table.shape[1]
    past = n_pages * PAGE
    assert past % MOBA_BLOCK == 0 and t <= MOBA_BLOCK and (t * N_HEADS) % SUBLANES == 0
    cache_fk, cache_fv, cache_logf, cache_mk, cache_mv = caches
    n_pool = cache_fk.shape[0]
    x2 = x.reshape(n, D_MODEL)
    pos = past + jnp.arange(t, dtype=I32)
    tabs = tuple(jnp.tile(tb_, (b, 1)) for tb_ in _rope_tables(pos))
    proj, logf = _inproj(x2, norm_mix[None, :], wts["w_main"], wts["w_f"], wts["b_f"], wts["colgain"],
                         tabs, tm=n)
    pt_flat = page_table.reshape(-1).astype(I32)
    rows = t * N_HEADS
    logf_bt = logf.reshape(b, t, LANES)[:, :, :N_HEADS]
    logf_new_t = jnp.pad(jnp.transpose(logf_bt, (0, 2, 1)), ((0, 0), (0, 0), (0, LANES - t)))
    nck_rep, c_new_t = _paged_cumsum(pt_flat, jnp.transpose(cache_logf, (0, 2, 1)), logf_new_t,
                                     n_pages=n_pages, pages=_pick(n_pages, 32))
    c_new_t = c_new_t[:, :, :t]
    cq32 = jnp.transpose(c_new_t, (0, 2, 1)).reshape(b, rows, 1)
    r_t = np.arange(rows)[:, None] // N_HEADS
    c_t = np.arange(t * N_KV)[None, :] // N_KV
    new_mask = _head_match_mask(rows, t * N_KV) + np.where(c_t <= r_t, 0.0, NEG).astype(np.float32)
    bias_new = jnp.tile(jnp.repeat(-c_new_t, N_KV, axis=-1), (1, t, 1)) + new_mask[None]
    page_mask = jnp.asarray(_head_match_mask(rows, PAGE_ROWS))
    blk_mask = jnp.asarray(_head_match_mask(rows, MOBA_BLOCK * N_KV))

    def q_rows(col):
        return proj[:, col:col + Q_W].reshape(b, rows, HEAD_DIM)

    def kv_rows(col):
        return proj[:, col:col + KV_W].reshape(b, t * N_KV, HEAD_DIM)

    page_rows = lambda c: c.reshape(n_pool, PAGE_ROWS, HEAD_DIM)
    oa = _fox_decode(pt_flat, q_rows(COL_FOX_Q), cq32, nck_rep, page_mask, kv_rows(COL_FOX_K),
                     kv_rows(COL_FOX_V), bias_new, page_rows(cache_fk), page_rows(cache_fv),
                     n_pages=n_pages, pages=_pick(n_pages, 16))
    ob = _moba_decode(pt_flat, q_rows(COL_MOBA_Q), blk_mask, kv_rows(COL_MOBA_K), kv_rows(COL_MOBA_V),
                      jnp.asarray(new_mask), page_rows(cache_mk), page_rows(cache_mv),
                      n_pages=n_pages, nb=_pick(n_pages * PAGE // MOBA_BLOCK, 8))
    x1 = _outproj(oa.reshape(n, Q_W), ob.reshape(n, Q_W), proj, x2, wts["wa"], wts["wb"], wts["wo"], tm=n)
    y = _moe(x1, norm_ffn[None, :], wts["wr2"], *experts, tm=n, tb=16)
    return y.reshape(b, t, D_MODEL), proj.reshape(b, t, PROJ_W), logf_bt


KV_COLS = (COL_FOX_K, COL_FOX_V, COL_MOBA_K, COL_MOBA_V)


def _kv_pack_body(*refs):
    srcs, dsts = refs[:len(KV_COLS)], refs[len(KV_COLS):]
    tm = srcs[0].shape[0]
    for src, dst in zip(srcs, dsts):
        for hh in range(N_KV):
            dst[pl.ds(hh, tm, stride=N_KV), :] = src[:, hh * HEAD_DIM:(hh + 1) * HEAD_DIM]


def _kv_pack(proj, *, tm):
    n = proj.shape[0]
    return pl.pallas_call(
        _kv_pack_body,
        grid=(n // tm,),
        in_specs=[pl.BlockSpec((tm, KV_W), lambda i, c=c: (i, c // KV_W)) for c in KV_COLS],
        out_specs=[pl.BlockSpec((tm * N_KV, HEAD_DIM), lambda i: (i, 0))] * len(KV_COLS),
        out_shape=[_sds((n * N_KV, HEAD_DIM), F32)] * len(KV_COLS),
        compiler_params=_cparams(("parallel",)),
        name="kv_pack",
    )(*([proj] * len(KV_COLS)))


def _kv_outputs(proj3, logf):
    b, t, _ = proj3.shape
    fk, fv, mk, mv = (a.reshape(1, b, t, N_KV, HEAD_DIM)
                      for a in _kv_pack(proj3.reshape(b * t, PROJ_W), tm=_pick(b * t, 512)))
    return fk, fv, logf[None], mk, mv


def kernel(x_prompt, x_sample, cache_fox_k, cache_fox_v, cache_fox_logf, cache_moba_k, cache_moba_v,
           page_table, norm_mix, w_in, b_forget, fox_q_norm, fox_k_norm, moba_q_norm, moba_k_norm,
           w_branch_fox, w_branch_moba, w_out, norm_ffn, w_router_group, w_router_expert,
           w_exp_gate, w_exp_up, w_exp_down):
    assert w_in.shape[0] == 1, "single-layer trunk"
    wts = _prep_weights(w_in[0], b_forget[0], (fox_q_norm[0], fox_k_norm[0], moba_q_norm[0], moba_k_norm[0]),
                        w_branch_fox[0], w_branch_moba[0], w_out[0], w_router_group[0], w_router_expert[0])
    experts = (w_exp_gate[0], w_exp_up[0], w_exp_down[0])
    caches = (cache_fox_k[0], cache_fox_v[0], cache_fox_logf[0], cache_moba_k[0], cache_moba_v[0])
    ys, proj_s, logf_s = _decode_layer(x_sample, wts, experts, norm_mix[0], norm_ffn[0], caches, page_table)
    yp, proj_p, logf_p = _prompt_layer(x_prompt, wts, experts, norm_mix[0], norm_ffn[0])
    return (yp, ys) + _kv_outputs(proj_p, logf_p) + _kv_outputs(proj_s, logf_s)
```

```python
import functools

import numpy as np
import jax
import jax.numpy as jnp
from jax import lax
from jax.experimental import pallas as pl
from jax.experimental.pallas import tpu as pltpu

F32 = jnp.float32
BF16 = jnp.bfloat16
I32 = jnp.int32
HIGHEST = lax.Precision.HIGHEST

LANES = 128
SUBLANES = 8
D_MODEL = 2048
HEAD_DIM = 128
N_HEADS = 8
N_KV = 4
GROUP = N_HEADS // N_KV
ROT_DIM = HEAD_DIM // 4
ROPE_THETA = 500000.0
ATTN_SCALE = HEAD_DIM ** -0.5
MOBA_BLOCK = 256
MOBA_TOPK = 3
PAGE = 128
PAGE_ROWS = PAGE * N_KV
N_GROUPS = 4
EXPERTS_PER_GROUP = 8
N_EXPERTS = N_GROUPS * EXPERTS_PER_GROUP
D_EXPERT = 512
RMS_EPS = 1e-6
NEG = -1e30
Q_W = N_HEADS * HEAD_DIM
KV_W = N_KV * HEAD_DIM
PROJ_W = 2 * (Q_W + 2 * KV_W) + 2 * D_MODEL
PROJ_TILE = 512
COL_FOX_Q, COL_FOX_K, COL_FOX_V = 0, Q_W, Q_W + KV_W
COL_MOBA_Q = Q_W + 2 * KV_W
COL_MOBA_K, COL_MOBA_V = COL_MOBA_Q + Q_W, COL_MOBA_Q + Q_W + KV_W
COL_GA = 2 * (Q_W + 2 * KV_W)
COL_GB = COL_GA + D_MODEL
VMEM_LIMIT = 56 * 1024 * 1024


def _cparams(sem):
    return pltpu.CompilerParams(dimension_semantics=sem, vmem_limit_bytes=VMEM_LIMIT)


def _sds(shape, dtype):
    return jax.ShapeDtypeStruct(shape, dtype)


def _head_rmsnorm(y, gain):
    ms = jnp.mean(y * y, axis=-1, keepdims=True)
    return y * lax.rsqrt(ms + RMS_EPS) * gain


def _inproj_body(x_ref, g_ref, w_ref, wf_ref, bf_ref, cg_ref, cos_ref, sa_ref, sb_ref,
                 proj_ref, logf_ref, h_scr):
    j = pl.program_id(1)

    @pl.when(j == 0)
    def _():
        x = x_ref[...]
        ms = jnp.mean(x * x, axis=-1, keepdims=True)
        h = (x * lax.rsqrt(ms + RMS_EPS) * g_ref[...]).astype(BF16)
        h_scr[...] = h
        z = jnp.dot(h, wf_ref[...], preferred_element_type=F32) + bf_ref[...]
        logf_ref[...] = -(jnp.maximum(-z, 0.0) + jnp.log1p(jnp.exp(-jnp.abs(z))))

    fox_normed = (j >= COL_FOX_Q // PROJ_TILE) & (j < COL_FOX_V // PROJ_TILE)
    moba_normed = (j >= COL_MOBA_Q // PROJ_TILE) & (j < COL_MOBA_V // PROJ_TILE)

    acc = jnp.dot(h_scr[...], w_ref[...], preferred_element_type=F32)

    @pl.when(jnp.logical_not(fox_normed | moba_normed))
    def _():
        proj_ref[...] = acc

    @pl.when(fox_normed)
    def _():
        for hh in range(PROJ_TILE // HEAD_DIM):
            sl = slice(hh * HEAD_DIM, (hh + 1) * HEAD_DIM)
            proj_ref[:, sl] = _head_rmsnorm(acc[:, sl], cg_ref[:, sl])

    @pl.when(moba_normed)
    def _():
        cos, sa, sb = cos_ref[...], sa_ref[...], sb_ref[...]
        half = ROT_DIM // 2
        for hh in range(PROJ_TILE // HEAD_DIM):
            sl = slice(hh * HEAD_DIM, (hh + 1) * HEAD_DIM)
            y = _head_rmsnorm(acc[:, sl], cg_ref[:, sl])
            proj_ref[:, sl] = (y * cos + pltpu.roll(y, half, axis=1) * sa
                               + pltpu.roll(y, HEAD_DIM - half, axis=1) * sb)


def _inproj(x2, g, w_main, w_f, b_f, colgain, tabs, *, tm):
    n = x2.shape[0]
    ntab = tabs[0].shape[0] // tm
    tab_spec = pl.BlockSpec((tm, LANES), lambda i, j: (i % ntab, 0))
    return pl.pallas_call(
        _inproj_body,
        grid=(n // tm, PROJ_W // PROJ_TILE),
        in_specs=[
            pl.BlockSpec((tm, D_MODEL), lambda i, j: (i, 0)),
            pl.BlockSpec((1, D_MODEL), lambda i, j: (0, 0)),
            pl.BlockSpec((D_MODEL, PROJ_TILE), lambda i, j: (0, j)),
            pl.BlockSpec((D_MODEL, LANES), lambda i, j: (0, 0)),
            pl.BlockSpec((1, LANES), lambda i, j: (0, 0)),
            pl.BlockSpec((1, PROJ_TILE), lambda i, j: (0, j)),
            tab_spec, tab_spec, tab_spec,
        ],
        out_specs=[pl.BlockSpec((tm, PROJ_TILE), lambda i, j: (i, j)),
                   pl.BlockSpec((tm, LANES), lambda i, j: (i, 0))],
        out_shape=[_sds((n, PROJ_W), F32), _sds((n, LANES), F32)],
        scratch_shapes=[pltpu.VMEM((tm, D_MODEL), BF16)],
        compiler_params=_cparams(("parallel", "arbitrary")),
        name="inproj",
    )(x2, g, w_main, w_f, b_f, colgain, *tabs)


def _lower_tri(n, strict=False):
    r = lax.broadcasted_iota(I32, (n, n), 0)
    c = lax.broadcasted_iota(I32, (n, n), 1)
    return (c < r) if strict else (c <= r)


def _cumsum_body(x_ref, o_ref, carry):
    @pl.when(pl.program_id(1) == 0)
    def _():
        carry[...] = jnp.zeros_like(carry)

    x = x_ref[0]
    tc = x.shape[0]
    tri = _lower_tri(tc).astype(F32)
    cs = jnp.dot(tri, x, precision=HIGHEST, preferred_element_type=F32) + carry[...]
    o_ref[0] = cs
    carry[...] = cs[tc - 1:tc, :]


def _cumsum(logf3, *, tc):
    b, t, w = logf3.shape
    return pl.pallas_call(
        _cumsum_body,
        grid=(b, t // tc),
        in_specs=[pl.BlockSpec((1, tc, w), lambda i, j: (i, j, 0))],
        out_specs=pl.BlockSpec((1, tc, w), lambda i, j: (i, j, 0)),
        out_shape=_sds((b, t, w), F32),
        scratch_shapes=[pltpu.VMEM((1, w), F32)],
        compiler_params=_cparams(("parallel", "arbitrary")),
        name="logf_cumsum",
    )(logf3)


def _dot_exact01(x, m01):
    hi = x.astype(BF16)
    r1 = x - hi.astype(F32)
    mid = r1.astype(BF16)
    lo = (r1 - mid.astype(F32)).astype(BF16)
    return (jnp.dot(hi, m01, preferred_element_type=F32) + jnp.dot(mid, m01, preferred_element_type=F32)
            + jnp.dot(lo, m01, preferred_element_type=F32))


def _paged_cumsum_body(pt_ref, *refs, pages):
    del pt_ref
    page_refs = refs[:pages]
    new_ref, cum_rep_ref, cum_ref = refs[pages:pages + 3]
    o_ref, onew_ref, carry = refs[pages + 3:]
    s = pl.program_id(1)

    @pl.when(s == 0)
    def _():
        carry[...] = jnp.zeros_like(carry)

    x = jnp.concatenate([r[0] for r in page_refs], axis=0)
    w = _dot_exact01(x, cum_rep_ref[...]).reshape(pages, N_HEADS, PAGE_ROWS)
    tot = w[:, :, PAGE_ROWS - 1:PAGE_ROWS]
    inc = tot
    sh = 1
    while sh < pages:
        inc = inc + jnp.concatenate([jnp.zeros((sh, N_HEADS, 1), F32), inc[:pages - sh]], axis=0)
        sh *= 2
    offs = jnp.concatenate([jnp.zeros((1, N_HEADS, 1), F32), inc[:pages - 1]], axis=0) + carry[...]
    for p in range(pages):
        o_ref[0, :, p * PAGE_ROWS:(p + 1) * PAGE_ROWS] = -(w[p] + offs[p])
    c = carry[...] + inc[pages - 1]
    carry[...] = c

    @pl.when(s == pl.num_programs(1) - 1)
    def _():
        onew_ref[0] = _dot_exact01(new_ref[0], cum_ref[...]) + c


def _paged_cumsum(pt_flat, cache_logf_t, logf_new_t, *, n_pages, pages):
    b = logf_new_t.shape[0]
    upper = np.triu(np.ones((PAGE, PAGE), np.float32))
    cum = jnp.asarray(upper, BF16)
    cum_rep = jnp.asarray(np.repeat(upper, N_KV, axis=1), BF16)

    def page_map(p):
        return lambda i, s, pt: (pt[i * n_pages + s * pages + p], 0, 0)

    const = lambda i, s, pt: (0, 0)
    grid_spec = pltpu.PrefetchScalarGridSpec(
        num_scalar_prefetch=1,
        grid=(b, n_pages // pages),
        in_specs=[pl.BlockSpec((1, N_HEADS, PAGE), page_map(p)) for p in range(pages)]
        + [pl.BlockSpec((1, N_HEADS, LANES), lambda i, s, pt: (i, 0, 0)),
           pl.BlockSpec((PAGE, PAGE_ROWS), const), pl.BlockSpec((PAGE, PAGE), const)],
        out_specs=[pl.BlockSpec((1, N_HEADS, pages * PAGE_ROWS), lambda i, s, pt: (i, 0, s)),
                   pl.BlockSpec((1, N_HEADS, LANES), lambda i, s, pt: (i, 0, 0))],
        scratch_shapes=[pltpu.VMEM((N_HEADS, 1), F32)],
    )
    return pl.pallas_call(
        functools.partial(_paged_cumsum_body, pages=pages),
        grid_spec=grid_spec,
        out_shape=[_sds((b, N_HEADS, n_pages * PAGE_ROWS), F32), _sds((b, N_HEADS, LANES), F32)],
        compiler_params=_cparams(("parallel", "arbitrary")),
        name="paged_logf_cumsum",
    )(pt_flat, *([cache_logf_t] * pages), logf_new_t, cum_rep, cum)


def _softmax_tile(t, rowshift, v, carry):
    m, l, acc = carry
    m_new = jnp.maximum(m, jnp.max(t, axis=-1, keepdims=True) + rowshift)
    p = jnp.exp(t + (rowshift - m_new))
    alpha = jnp.exp(m - m_new)
    l = alpha * l + jnp.sum(p, axis=-1, keepdims=True)
    acc = alpha * acc + jnp.dot(p.astype(BF16), v, preferred_element_type=F32)
    return m_new, l, acc


def _softmax_init(rows):
    return (jnp.full((rows, 1), NEG, F32), jnp.zeros((rows, 1), F32), jnp.zeros((rows, HEAD_DIM), F32))


def _qk(q, k):
    return lax.dot_general(q, k, (((1,), (1,)), ((), ())), preferred_element_type=F32)


def _fox_body(q_ref, k_ref, v_ref, cq_ref, ck_ref, o_ref, *, tq):
    h = pl.program_id(1)
    qi = pl.program_id(2)
    lane = lax.broadcasted_iota(I32, (tq, LANES), 1)
    qs, cqs = [], []
    for g in range(GROUP):
        qs.append((q_ref[0, :, g * HEAD_DIM:(g + 1) * HEAD_DIM] * ATTN_SCALE).astype(BF16))
        cqs.append(jnp.sum(jnp.where(lane == h * GROUP + g, cq_ref[0], 0.0), axis=-1, keepdims=True))

    def tile(start, diag, carries):
        k = k_ref[0, pl.ds(start, tq), :].astype(BF16)
        v = v_ref[0, pl.ds(start, tq), :].astype(BF16)
        out = []
        for g in range(GROUP):
            ck = ck_ref[0, pl.ds(h * GROUP + g, 1), pl.ds(start, tq)]
            t = _qk(qs[g], k) - ck
            if diag:
                t = jnp.where(_lower_tri(tq), t, NEG)
            out.append(_softmax_tile(t, cqs[g], v, carries[g]))
        return tuple(out)

    carries = tile(pl.multiple_of(qi * tq, tq), True, (_softmax_init(tq),) * GROUP)
    carries = lax.fori_loop(0, qi, lambda j, c: tile(pl.multiple_of(j * tq, tq), False, c), carries)
    for g in range(GROUP):
        _, l, acc = carries[g]
        o_ref[0, :, g * HEAD_DIM:(g + 1) * HEAD_DIM] = (acc / l).astype(o_ref.dtype)


def _fox_prompt(proj3, c3, ck_t, *, tq):
    b, t, _ = proj3.shape
    gw = GROUP * HEAD_DIM
    qb, kb, vb = COL_FOX_Q // gw, COL_FOX_K // HEAD_DIM, COL_FOX_V // HEAD_DIM
    return pl.pallas_call(
        functools.partial(_fox_body, tq=tq),
        grid=(b, N_KV, t // tq),
        in_specs=[
            pl.BlockSpec((1, tq, gw), lambda i, h, q: (i, q, qb + h)),
            pl.BlockSpec((1, t, HEAD_DIM), lambda i, h, q: (i, 0, kb + h)),
            pl.BlockSpec((1, t, HEAD_DIM), lambda i, h, q: (i, 0, vb + h)),
            pl.BlockSpec((1, tq, LANES), lambda i, h, q: (i, q, 0)),
            pl.BlockSpec((1, N_HEADS, t), lambda i, h, q: (i, 0, 0)),
        ],
        out_specs=pl.BlockSpec((1, tq, gw), lambda i, h, q: (i, q, h)),
        out_shape=_sds((b, t, Q_W), BF16),
        compiler_params=_cparams(("parallel", "parallel", "arbitrary")),
        name="fox_prompt",
    )(proj3, proj3, proj3, c3, ck_t)


def _topk_select(g, elig, n_cand, axis):
    idx = lax.broadcasted_iota(I32, g.shape, axis)
    g = jnp.where(elig, g, -jnp.inf)
    cnt = jnp.zeros(g.shape, F32)
    for m in range(n_cand):
        gm = g[m:m + 1, :] if axis == 0 else g[:, m:m + 1]
        beats = (gm > g) | ((gm == g) & (m < idx))
        cnt = cnt + jnp.where(beats, 1.0, 0.0)
    return elig & (cnt < MOBA_TOPK)


def _moba_body(q_ref, qall_ref, k_ref, v_ref, o_ref, kmean_scr, sel_scr, *, n_blk):
    qi = pl.program_id(2)
    tq = MOBA_BLOCK
    nb8 = kmean_scr.shape[0]
    t_all = qall_ref.shape[1]

    @pl.when(qi == 0)
    def _():
        kmean_scr[...] = jnp.zeros_like(kmean_scr)
        for n in range(n_blk):
            kb = k_ref[0, n * tq:(n + 1) * tq, :]
            kmean_scr[n:n + 1, :] = jnp.mean(kb, axis=0, keepdims=True)
        eye = jnp.where(lax.broadcasted_iota(I32, (tq, tq), 0) == lax.broadcasted_iota(I32, (tq, tq), 1),
                        1.0, 0.0).astype(BF16)
        blk = lax.broadcasted_iota(I32, (nb8, t_all), 0)
        q_blk = lax.shift_right_logical(lax.broadcasted_iota(I32, (nb8, t_all), 1),
                                        MOBA_BLOCK.bit_length() - 1)
        for g in range(GROUP):
            gate_t = lax.dot_general(kmean_scr[...], qall_ref[0, :, g * HEAD_DIM:(g + 1) * HEAD_DIM],
                                     (((1,), (1,)), ((), ())), precision=HIGHEST, preferred_element_type=F32)
            sel_t = jnp.where(_topk_select(gate_t, blk < q_blk, n_blk, 0), 1.0, 0.0).astype(BF16)
            for c in range(t_all // tq):
                sel_scr[g, c * tq:(c + 1) * tq, :] = _qk(eye, sel_t[:, c * tq:(c + 1) * tq])

    lane_blk = lax.broadcasted_iota(I32, (tq, nb8), 1)
    qs, sels = [], []
    for g in range(GROUP):
        qs.append((q_ref[0, :, g * HEAD_DIM:(g + 1) * HEAD_DIM] * ATTN_SCALE).astype(BF16))
        sels.append(sel_scr[g, pl.ds(pl.multiple_of(qi * tq, tq), tq), :])

    def load(start, size):
        return (k_ref[0, pl.ds(start, size), :].astype(BF16), v_ref[0, pl.ds(start, size), :].astype(BF16))

    def row_bias(g, n):
        flag = jnp.sum(jnp.where(lane_blk == n, sels[g], 0.0), axis=-1, keepdims=True)
        return (flag - 1.0) * (-NEG)

    def past(n, carries):
        k, v = load(pl.multiple_of(n * tq, tq), tq)
        return tuple(_softmax_tile(_qk(qs[g], k) + row_bias(g, n), 0.0, v, carries[g])
                     for g in range(GROUP))

    first_half = lax.broadcasted_iota(I32, (tq, 2 * tq), 1) < tq

    def past_pair(j, carries):
        k, v = load(pl.multiple_of(2 * j * tq, 2 * tq), 2 * tq)
        out = []
        for g in range(GROUP):
            bias = jnp.where(first_half, row_bias(g, 2 * j), row_bias(g, 2 * j + 1))
            out.append(_softmax_tile(_qk(qs[g], k) + bias, 0.0, v, carries[g]))
        return tuple(out)

    k, v = load(pl.multiple_of(qi * tq, tq), tq)
    carries = tuple(
        _softmax_tile(jnp.where(_lower_tri(tq), _qk(qs[g], k), NEG), 0.0, v, _softmax_init(tq))
        for g in range(GROUP))
    carries = lax.fori_loop(0, lax.shift_right_logical(qi, 1), past_pair, carries)
    carries = lax.cond((qi & 1) == 1, lambda c: past(qi - 1, c), lambda c: c, carries)
    for g in range(GROUP):
        _, l, acc = carries[g]
        o_ref[0, :, g * HEAD_DIM:(g + 1) * HEAD_DIM] = (acc / l).astype(o_ref.dtype)


def _moba_prompt(proj3):
    b, t, _ = proj3.shape
    n_blk = t // MOBA_BLOCK
    assert t % MOBA_BLOCK == 0
    nb8 = -(-n_blk // SUBLANES) * SUBLANES
    gw = GROUP * HEAD_DIM
    qb, kb, vb = COL_MOBA_Q // gw, COL_MOBA_K // HEAD_DIM, COL_MOBA_V // HEAD_DIM
    tq = MOBA_BLOCK
    return pl.pallas_call(
        functools.partial(_moba_body, n_blk=n_blk),
        grid=(b, N_KV, t // tq),
        in_specs=[
            pl.BlockSpec((1, tq, gw), lambda i, h, q: (i, q, qb + h)),
            pl.BlockSpec((1, t, gw), lambda i, h, q: (i, 0, qb + h)),
            pl.BlockSpec((1, t, HEAD_DIM), lambda i, h, q: (i, 0, kb + h)),
            pl.BlockSpec((1, t, HEAD_DIM), lambda i, h, q: (i, 0, vb + h)),
        ],
        out_specs=pl.BlockSpec((1, tq, gw), lambda i, h, q: (i, q, h)),
        out_shape=_sds((b, t, Q_W), BF16),
        scratch_shapes=[pltpu.VMEM((nb8, HEAD_DIM), F32), pltpu.VMEM((GROUP, t, nb8), F32)],
        compiler_params=_cparams(("parallel", "parallel", "arbitrary")),
        name="moba_prompt",
    )(proj3, proj3, proj3, proj3)


def _fox_decode_body(pt_ref, *refs, pages, rows):
    del pt_ref
    q_ref, cq_ref, nck_ref, mask_ref, knew_ref, vnew_ref, bnew_ref = refs[:7]
    k_refs = refs[7:7 + pages]
    v_refs = refs[7 + pages:7 + 2 * pages]
    o_ref, m_scr, l_scr, acc_scr = refs[7 + 2 * pages:]
    s = pl.program_id(1)

    @pl.when(s == 0)
    def _():
        m_scr[...] = jnp.full_like(m_scr, NEG)
        l_scr[...] = jnp.zeros_like(l_scr)
        acc_scr[...] = jnp.zeros_like(acc_scr)

    q = (q_ref[0] * ATTN_SCALE).astype(BF16)
    cq = cq_ref[0]
    m, l, acc = m_scr[...], l_scr[...], acc_scr[...]
    t = jnp.concatenate([_qk(q, k_refs[p][0].astype(BF16)) for p in range(pages)], axis=1)
    t = t + jnp.concatenate([nck_ref[0]] * (rows // N_HEADS), axis=0) + mask_ref[...]
    m_new = jnp.maximum(m, jnp.max(t, axis=-1, keepdims=True) + cq)
    pf = jnp.exp(t + (cq - m_new))
    alpha = jnp.exp(m - m_new)
    l = alpha * l + jnp.sum(pf, axis=-1, keepdims=True)
    p_ = pf.astype(BF16)
    pv = jnp.dot(p_[:, 0:PAGE_ROWS], v_refs[0][0].astype(BF16), preferred_element_type=F32)
    for p in range(1, pages):
        pv = pv + jnp.dot(p_[:, p * PAGE_ROWS:(p + 1) * PAGE_ROWS], v_refs[p][0].astype(BF16),
                          preferred_element_type=F32)
    acc = alpha * acc + pv
    m_scr[...], l_scr[...], acc_scr[...] = m_new, l, acc

    @pl.when(s == pl.num_programs(1) - 1)
    def _():
        tn = _qk(q, knew_ref[0].astype(BF16)) + bnew_ref[0]
        _, l2, acc2 = _softmax_tile(tn, cq, vnew_ref[0].astype(BF16), (m_new, l, acc))
        o_ref[0] = (acc2 / l2).astype(o_ref.dtype)


def _fox_decode(pt_flat, q32, cq32, nck_rep, mask, k_new, v_new, bias_new, cache_k, cache_v,
                *, n_pages, pages):
    b, rows, _ = q32.shape
    n_new = k_new.shape[1]

    def page_map(p):
        return lambda i, s, pt: (pt[i * n_pages + s * pages + p], 0, 0)

    per_b = lambda i, s, pt: (i, 0, 0)
    grid_spec = pltpu.PrefetchScalarGridSpec(
        num_scalar_prefetch=1,
        grid=(b, n_pages // pages),
        in_specs=[
            pl.BlockSpec((1, rows, HEAD_DIM), per_b),
            pl.BlockSpec((1, rows, 1), per_b),
            pl.BlockSpec((1, N_HEADS, pages * PAGE_ROWS), lambda i, s, pt: (i, 0, s)),
            pl.BlockSpec((rows, pages * PAGE_ROWS), lambda i, s, pt: (0, 0)),
            pl.BlockSpec((1, n_new, HEAD_DIM), per_b),
            pl.BlockSpec((1, n_new, HEAD_DIM), per_b),
            pl.BlockSpec((1, rows, n_new), per_b),
        ] + [pl.BlockSpec((1, PAGE_ROWS, HEAD_DIM), page_map(p)) for p in range(pages)] * 2,
        out_specs=pl.BlockSpec((1, rows, HEAD_DIM), per_b),
        scratch_shapes=[pltpu.VMEM((rows, 1), F32), pltpu.VMEM((rows, 1), F32),
                        pltpu.VMEM((rows, HEAD_DIM), F32)],
    )
    return pl.pallas_call(
        functools.partial(_fox_decode_body, pages=pages, rows=rows),
        grid_spec=grid_spec,
        out_shape=_sds((b, rows, HEAD_DIM), BF16),
        compiler_params=_cparams(("parallel", "arbitrary")),
        name="fox_decode",
    )(pt_flat, q32, cq32, nck_rep, jnp.tile(mask, (1, pages)), k_new, v_new, bias_new,
      *([cache_k] * pages), *([cache_v] * pages))


def _tree_sum(x):
    while x.shape[0] > 1:
        half = x.shape[0] // 2
        x = x[:half] + x[half:]
    return x[0]


def _moba_decode_body(pt_ref, *refs, nb, n_past, rows):
    del pt_ref
    ppb = MOBA_BLOCK // PAGE
    pages = nb * ppb
    q_ref, mask_ref, knew_ref, vnew_ref, bnew_ref = refs[:5]
    k_refs = refs[5:5 + pages]
    v_refs = refs[5 + pages:5 + 2 * pages]
    o_ref, m_scr, l_scr, g_scr, acc_scr = refs[5 + 2 * pages:]
    s = pl.program_id(1)

    @pl.when(s == 0)
    def _():
        m_scr[...] = jnp.zeros_like(m_scr)
        l_scr[...] = jnp.zeros_like(l_scr)
        g_scr[...] = jnp.zeros_like(g_scr)

    qf = q_ref[0]
    q = (qf * ATTN_SCALE).astype(BF16)
    lane = lax.broadcasted_iota(I32, (rows, LANES), 1)
    m_all, l_all, g_all = m_scr[...], l_scr[...], g_scr[...]
    for j in range(nb):
        n = s * nb + j
        kf = jnp.concatenate([k_refs[j * ppb + p][0] for p in range(ppb)], axis=0)
        vf = jnp.concatenate([v_refs[j * ppb + p][0] for p in range(ppb)], axis=0)
        t = _qk(q, kf.astype(BF16)) + mask_ref[...]
        m_n = jnp.max(t, axis=-1, keepdims=True)
        p_ = jnp.exp(t - m_n)
        l_n = jnp.sum(p_, axis=-1, keepdims=True)
        acc_scr[n] = jnp.dot(p_.astype(BF16), vf.astype(BF16), preferred_element_type=F32)
        s8 = _tree_sum(kf.reshape(MOBA_BLOCK * N_KV // SUBLANES, SUBLANES, HEAD_DIM))
        ks = s8[0:N_KV] + s8[N_KV:2 * N_KV]
        ks_h = jnp.concatenate([ks[h // GROUP:h // GROUP + 1] for h in range(N_HEADS)], axis=0)
        ks_r = jnp.concatenate([ks_h] * (rows // N_HEADS), axis=0)
        g_n = jnp.sum(qf * ks_r, axis=-1, keepdims=True) * (1.0 / MOBA_BLOCK)
        m_all = jnp.where(lane == n, m_n, m_all)
        l_all = jnp.where(lane == n, l_n, l_all)
        g_all = jnp.where(lane == n, g_n, g_all)
    m_scr[...], l_scr[...], g_scr[...] = m_all, l_all, g_all

    @pl.when(s == pl.num_programs(1) - 1)
    def _():
        t = _qk(q, knew_ref[0].astype(BF16)) + bnew_ref[...]
        m_own = jnp.max(t, axis=-1, keepdims=True)
        p_own = jnp.exp(t - m_own)
        l_own = jnp.sum(p_own, axis=-1, keepdims=True)
        acc_own = jnp.dot(p_own.astype(BF16), vnew_ref[0].astype(BF16), preferred_element_type=F32)
        sel = _topk_select(g_all, lane < n_past, n_past, 1)
        m_x = jnp.maximum(m_own, jnp.max(jnp.where(sel, m_all, NEG), axis=-1, keepdims=True))
        w = jnp.where(sel, jnp.exp(m_all - m_x), 0.0)
        w_own = jnp.exp(m_own - m_x)
        den = w_own * l_own + jnp.sum(w * l_all, axis=-1, keepdims=True)
        num = w_own * acc_own
        for n in range(n_past):
            num = num + w[:, n:n + 1] * acc_scr[n]
        o_ref[0] = (num / den).astype(o_ref.dtype)


def _moba_decode(pt_flat, q32, mask, k_new, v_new, bias_new, cache_k, cache_v, *, n_pages, nb):
    b, rows, _ = q32.shape
    ppb = MOBA_BLOCK // PAGE
    pages = nb * ppb
    n_past = n_pages // ppb
    assert n_past <= LANES and n_past % nb == 0
    n_new = k_new.shape[1]

    def page_map(p):
        return lambda i, s, pt: (pt[i * n_pages + s * pages + p], 0, 0)

    per_b = lambda i, s, pt: (i, 0, 0)
    grid_spec = pltpu.PrefetchScalarGridSpec(
        num_scalar_prefetch=1,
        grid=(b, n_past // nb),
        in_specs=[
            pl.BlockSpec((1, rows, HEAD_DIM), per_b),
            pl.BlockSpec((rows, ppb * PAGE_ROWS), lambda i, s, pt: (0, 0)),
            pl.BlockSpec((1, n_new, HEAD_DIM), per_b),
            pl.BlockSpec((1, n_new, HEAD_DIM), per_b),
            pl.BlockSpec((rows, n_new), lambda i, s, pt: (0, 0)),
        ] + [pl.BlockSpec((1, PAGE_ROWS, HEAD_DIM), page_map(p)) for p in range(pages)] * 2,
        out_specs=pl.BlockSpec((1, rows, HEAD_DIM), per_b),
        scratch_shapes=[pltpu.VMEM((rows, LANES), F32), pltpu.VMEM((rows, LANES), F32),
                        pltpu.VMEM((rows, LANES), F32), pltpu.VMEM((n_past, rows, HEAD_DIM), F32)],
    )
    return pl.pallas_call(
        functools.partial(_moba_decode_body, nb=nb, n_past=n_past, rows=rows),
        grid_spec=grid_spec,
        out_shape=_sds((b, rows, HEAD_DIM), BF16),
        compiler_params=_cparams(("parallel", "arbitrary")),
        name="moba_decode",
    )(pt_flat, q32, mask, k_new, v_new, bias_new, *([cache_k] * pages), *([cache_v] * pages))


def _outproj_body(oa_ref, ob_ref, ga_ref, gb_ref, x_ref, wa_ref, wb_ref, wo_ref, o_ref):
    ya = jnp.dot(oa_ref[...], wa_ref[...], preferred_element_type=F32)
    yb = jnp.dot(ob_ref[...], wb_ref[...], preferred_element_type=F32)
    mixed = jax.nn.sigmoid(ga_ref[...]) * ya + jax.nn.sigmoid(gb_ref[...]) * yb
    o_ref[...] = x_ref[...] + jnp.dot(mixed.astype(BF16), wo_ref[...], preferred_element_type=F32)


def _outproj(oa, ob, proj, x2, wa, wb, wo, *, tm):
    n = x2.shape[0]
    const = lambda i: (0, 0)
    once = pl.Buffered(1)
    return pl.pallas_call(
        _outproj_body,
        grid=(n // tm,),
        in_specs=[
            pl.BlockSpec((tm, Q_W), lambda i: (i, 0)),
            pl.BlockSpec((tm, Q_W), lambda i: (i, 0)),
            pl.BlockSpec((tm, D_MODEL), lambda i: (i, COL_GA // D_MODEL)),
            pl.BlockSpec((tm, D_MODEL), lambda i: (i, COL_GB // D_MODEL)),
            pl.BlockSpec((tm, D_MODEL), lambda i: (i, 0)),
            pl.BlockSpec((Q_W, D_MODEL), const, pipeline_mode=once),
            pl.BlockSpec((Q_W, D_MODEL), const, pipeline_mode=once),
            pl.BlockSpec((D_MODEL, D_MODEL), const, pipeline_mode=once),
        ],
        out_specs=pl.BlockSpec((tm, D_MODEL), lambda i: (i, 0)),
        out_shape=_sds((n, D_MODEL), F32),
        compiler_params=_cparams(("parallel",)),
        name="outproj",
    )(oa, ob, proj, proj, x2, wa, wb, wo)


def _first_lane(hit, lane):
    return jnp.min(jnp.where(hit, lane.astype(F32), float(LANES)), axis=-1, keepdims=True).astype(I32)


def _router_body(x_ref, g_ref, wr_ref, hn_ref, ii_ref, iw_ref, cnt_ref, cnt_scr):
    @pl.when(pl.program_id(0) == 0)
    def _():
        cnt_scr[...] = jnp.zeros_like(cnt_scr)

    x = x_ref[...]
    tm = x.shape[0]
    ms = jnp.mean(x * x, axis=-1, keepdims=True)
    hn = x * lax.rsqrt(ms + RMS_EPS) * g_ref[...]
    hn_ref[...] = hn
    h_hi = hn.astype(BF16)
    h_lo = (hn - h_hi.astype(F32)).astype(BF16)
    quad = jnp.dot(jnp.concatenate([h_hi, h_lo], axis=0), wr_ref[...], preferred_element_type=F32)
    logits = (quad[:tm, :LANES] + quad[:tm, LANES:]) + (quad[tm:, :LANES] + quad[tm:, LANES:])
    lane = lax.broadcasted_iota(I32, (tm, LANES), 1)
    is_g = lane < N_GROUPS
    gl = jnp.where(is_g, logits, -jnp.inf)
    gmax = jnp.max(gl, axis=-1, keepdims=True)
    grp = _first_lane(gl == gmax, lane)
    p_grp = 1.0 / jnp.sum(jnp.where(is_g, jnp.exp(gl - gmax), 0.0), axis=-1, keepdims=True)
    e_lane = lane - N_GROUPS
    shift = EXPERTS_PER_GROUP.bit_length() - 1
    in_grp = (e_lane < N_EXPERTS) & (lax.shift_right_arithmetic(e_lane, shift) == grp)
    el = jnp.where(in_grp, logits, -jnp.inf)
    v1 = jnp.max(el, axis=-1, keepdims=True)
    i1 = _first_lane(el == v1, lane)
    el2 = jnp.where(lane == i1, -jnp.inf, el)
    v2 = jnp.max(el2, axis=-1, keepdims=True)
    i2 = _first_lane(el2 == v2, lane)
    r = jnp.exp(v2 - v1)
    w1 = p_grp / (1.0 + r)
    w2 = p_grp * r / (1.0 + r)
    e1 = i1 - N_GROUPS
    e2 = i2 - N_GROUPS
    oh = jnp.where((lane == e1) | (lane == e2), 1.0, 0.0)
    before = cnt_scr[...] + jnp.dot(_lower_tri(tm, strict=True).astype(BF16), oh.astype(BF16),
                                    preferred_element_type=F32)
    r1 = jnp.sum(jnp.where(lane == e1, before, 0.0), axis=-1, keepdims=True)
    r2 = jnp.sum(jnp.where(lane == e2, before, 0.0), axis=-1, keepdims=True)
    cnt_scr[...] = cnt_scr[...] + jnp.sum(oh, axis=0, keepdims=True)
    cnt_ref[...] = cnt_scr[...].astype(I32)
    ii = jnp.where(lane == 0, e1, jnp.where(lane == 1, e2, jnp.where(
        lane == 2, r1.astype(I32), jnp.where(lane == 3, r2.astype(I32), 0))))
    ii_ref[...] = ii
    iw_ref[...] = jnp.where(lane == 0, w1, jnp.where(lane == 1, w2, 0.0))


def _router(x1, g, wr2, *, tm):
    n = x1.shape[0]
    return pl.pallas_call(
        _router_body,
        grid=(n // tm,),
        in_specs=[pl.BlockSpec((tm, D_MODEL), lambda i: (i, 0)),
                  pl.BlockSpec((1, D_MODEL), lambda i: (0, 0)),
                  pl.BlockSpec((D_MODEL, 2 * LANES), lambda i: (0, 0))],
        out_specs=[pl.BlockSpec((tm, D_MODEL), lambda i: (i, 0)),
                   pl.BlockSpec((tm, LANES), lambda i: (i, 0)),
                   pl.BlockSpec((tm, LANES), lambda i: (i, 0)),
                   pl.BlockSpec((1, LANES), lambda i: (0, 0))],
        out_shape=[_sds((n, D_MODEL), F32), _sds((n, LANES), I32), _sds((n, LANES), F32),
                   _sds((1, LANES), I32)],
        scratch_shapes=[pltpu.VMEM((1, LANES), F32)],
        compiler_params=_cparams(("arbitrary",)),
        name="moe_router",
    )(x1, g, wr2)


def _row_token_body(dest_ref, tok_ref, *, n_tok, n_rows):
    def zero(i, c):
        tok_ref[i] = 0
        return c

    lax.fori_loop(0, n_rows, zero, 0, unroll=8)

    def put(t, c):
        tok_ref[dest_ref[t]] = t
        tok_ref[dest_ref[n_tok + t]] = t
        return c

    lax.fori_loop(0, n_tok, put, 0, unroll=8)


def _row_token(dest_flat, *, n_rows):
    n_tok = dest_flat.shape[0] // 2
    return pl.pallas_call(
        functools.partial(_row_token_body, n_tok=n_tok, n_rows=n_rows),
        in_specs=[pl.BlockSpec(memory_space=pltpu.SMEM)],
        out_specs=pl.BlockSpec(memory_space=pltpu.SMEM),
        out_shape=_sds((n_rows,), I32),
        name="moe_row_token",
    )(dest_flat)


GATHER_UNROLL = 8


def _row_gather_start(src_hbm, idx_ref, base, buf, row0, sem, n):
    def body(r, c):
        pltpu.make_async_copy(src_hbm.at[pl.ds(idx_ref[base + r], 1)], buf.at[pl.ds(row0 + r, 1)],
                              sem).start()
        return c

    lax.fori_loop(0, n, body, 0, unroll=GATHER_UNROLL)


def _row_gather_wait(src_hbm, buf, sem):
    pltpu.make_async_copy(src_hbm.at[pl.ds(0, buf.shape[0])], buf, sem).wait()


def _experts_body(be_ref, nu_ref, tok_ref, hn_hbm, wg_ref, wu_ref, wd_ref, y_ref,
                  xbuf, wg_s, wu_s, wd_s, sems, *, tb):
    i = pl.program_id(0)
    n_used = nu_ref[0]
    slot = i % 2

    @pl.when(i == 0)
    def _():
        _row_gather_start(hn_hbm, tok_ref, 0, xbuf.at[0], 0, sems.at[0], tb)

    @pl.when(i + 1 < n_used)
    def _():
        _row_gather_start(hn_hbm, tok_ref, (i + 1) * tb, xbuf.at[1 - slot], 0, sems.at[1 - slot], tb)

    new_expert = (i == 0) | (be_ref[i] != be_ref[jnp.maximum(i - 1, 0)])

    @pl.when((i < n_used) & new_expert)
    def _():
        wg_s[...] = wg_ref[0].astype(BF16)
        wu_s[...] = wu_ref[0].astype(BF16)
        wd_s[...] = wd_ref[0].astype(BF16)

    @pl.when(i < n_used)
    def _():
        _row_gather_wait(hn_hbm, xbuf.at[slot], sems.at[slot])
        x = xbuf[slot].astype(BF16)
        hg = jnp.dot(x, wg_s[...], preferred_element_type=F32)
        hu = jnp.dot(x, wu_s[...], preferred_element_type=F32)
        hid = (hg * jax.nn.sigmoid(hg) * hu).astype(BF16)
        y_ref[...] = jnp.dot(hid, wd_s[...], preferred_element_type=F32)

    @pl.when(i >= n_used)
    def _():
        y_ref[...] = jnp.zeros_like(y_ref)


def _experts(blk_expert, n_used, row_tok, hn, wg, wu, wd, *, tb):
    n_blocks = blk_expert.shape[0]
    wmap = lambda i, be, nu, tok: (be[i], 0, 0)
    grid_spec = pltpu.PrefetchScalarGridSpec(
        num_scalar_prefetch=3,
        grid=(n_blocks,),
        in_specs=[pl.BlockSpec(memory_space=pl.ANY),
                  pl.BlockSpec((1, D_MODEL, D_EXPERT), wmap),
                  pl.BlockSpec((1, D_MODEL, D_EXPERT), wmap),
                  pl.BlockSpec((1, D_EXPERT, D_MODEL), wmap)],
        out_specs=pl.BlockSpec((tb, D_MODEL), lambda i, be, nu, tok: (i, 0)),
        scratch_shapes=[pltpu.VMEM((2, tb, D_MODEL), F32),
                        pltpu.VMEM((D_MODEL, D_EXPERT), BF16), pltpu.VMEM((D_MODEL, D_EXPERT), BF16),
                        pltpu.VMEM((D_EXPERT, D_MODEL), BF16), pltpu.SemaphoreType.DMA((2,))],
    )
    return pl.pallas_call(
        functools.partial(_experts_body, tb=tb),
        grid_spec=grid_spec,
        out_shape=_sds((n_blocks * tb, D_MODEL), F32),
        compiler_params=_cparams(("arbitrary",)),
        name="moe_experts",
    )(blk_expert, n_used, row_tok, hn, wg, wu, wd)


def _combine_body(dest_ref, x_ref, iw_ref, y_hbm, o_ref, ybuf, sems, *, tm, n):
    i = pl.program_id(0)
    n_steps = pl.num_programs(0)
    slot = i % 2

    def start(step, to_slot):
        for k in range(2):
            _row_gather_start(y_hbm, dest_ref, k * n + step * tm, ybuf.at[to_slot], k * tm,
                              sems.at[to_slot], tm)

    @pl.when(i == 0)
    def _():
        start(0, 0)

    @pl.when(i + 1 < n_steps)
    def _():
        start(i + 1, 1 - slot)

    _row_gather_wait(y_hbm, ybuf.at[slot], sems.at[slot])
    iw = iw_ref[...]
    o_ref[...] = (x_ref[...] + iw[:, 0:1] * ybuf[slot, 0:tm, :] + iw[:, 1:2] * ybuf[slot, tm:2 * tm, :])


def _combine(dest_flat, x1, iw, y, *, tm):
    n = x1.shape[0]
    grid_spec = pltpu.PrefetchScalarGridSpec(
        num_scalar_prefetch=1,
        grid=(n // tm,),
        in_specs=[pl.BlockSpec((tm, D_MODEL), lambda i, d: (i, 0)),
                  pl.BlockSpec((tm, LANES), lambda i, d: (i, 0)),
                  pl.BlockSpec(memory_space=pl.ANY)],
        out_specs=pl.BlockSpec((tm, D_MODEL), lambda i, d: (i, 0)),
        scratch_shapes=[pltpu.VMEM((2, 2 * tm, D_MODEL), F32), pltpu.SemaphoreType.DMA((2,))],
    )
    return pl.pallas_call(
        functools.partial(_combine_body, tm=tm, n=n),
        grid_spec=grid_spec,
        out_shape=_sds((n, D_MODEL), F32),
        compiler_params=_cparams(("arbitrary",)),
        name="moe_combine",
    )(dest_flat, x1, iw, y)


def _moe(x1, norm_g, wr2, wg, wu, wd, *, tm, tb):
    n = x1.shape[0]
    hn, ii, iw, cnt = _router(x1, norm_g, wr2, tm=tm)
    counts = cnt[0, :N_EXPERTS]
    padded = (counts + tb - 1) // tb * tb
    pend = jnp.cumsum(padded)
    pstart = pend - padded
    n_blocks = -(-2 * n // tb) + N_EXPERTS
    eids = jnp.arange(N_EXPERTS, dtype=I32)
    e12 = ii[:, 0:2]
    start12 = jnp.sum(jnp.where(e12[:, :, None] == eids, pstart, 0), axis=-1)
    dest_flat = (start12 + ii[:, 2:4]).T.reshape(-1).astype(I32)
    blk_start = jnp.arange(n_blocks, dtype=I32) * tb
    blk_expert = jnp.minimum(jnp.sum(blk_start[:, None] >= pend[None, :], axis=-1), N_EXPERTS - 1).astype(I32)
    n_used = (pend[-1:] // tb).astype(I32)
    row_tok = _row_token(dest_flat, n_rows=n_blocks * tb)
    y = _experts(blk_expert, n_used, row_tok, hn, wg, wu, wd, tb=tb)
    return _combine(dest_flat, x1, iw, y, tm=tm)


def _rope_tables(pos):
    half = ROT_DIM // 2
    inv = jnp.power(ROPE_THETA, -jnp.arange(0, ROT_DIM, 2, dtype=F32) / ROT_DIM)
    ang = pos.astype(F32)[:, None] * inv[None, :]
    cos, sin = jnp.cos(ang), jnp.sin(ang)
    t = pos.shape[0]
    rest = HEAD_DIM - ROT_DIM
    cos_t = jnp.concatenate([cos, cos, jnp.ones((t, rest), F32)], axis=1)
    sa = jnp.concatenate([jnp.zeros((t, half), F32), sin, jnp.zeros((t, rest), F32)], axis=1)
    sb = jnp.concatenate([-sin, jnp.zeros((t, half + rest), F32)], axis=1)
    return cos_t, sa, sb


def _head_match_mask(rows, cols):
    r = np.arange(rows)[:, None]
    c = np.arange(cols)[None, :]
    return np.where((c % N_KV) == ((r % N_HEADS) // GROUP), 0.0, NEG).astype(np.float32)


def _prep_weights(w_in, b_forget, gains, w_branch_fox, w_branch_moba, w_out, w_router_group,
                  w_router_expert):
    fq, fk, mq, mk = gains
    off_f = Q_W + 2 * KV_W
    w_main = jnp.concatenate([w_in[:, :off_f], w_in[:, off_f + N_HEADS:]], axis=1).astype(BF16)
    w_f = jnp.pad(w_in[:, off_f:off_f + N_HEADS], ((0, 0), (0, LANES - N_HEADS))).astype(BF16)
    b_f = jnp.pad(b_forget, (0, LANES - N_HEADS))[None, :]
    ones = lambda w: jnp.ones((w,), F32)
    colgain = jnp.concatenate([
        jnp.tile(fq, N_HEADS), jnp.tile(fk, N_KV), ones(KV_W),
        jnp.tile(mq, N_HEADS), jnp.tile(mk, N_KV), ones(KV_W), ones(2 * D_MODEL)])[None, :]
    wr = jnp.pad(jnp.concatenate([w_router_group, w_router_expert], axis=1),
                 ((0, 0), (0, LANES - N_GROUPS - N_EXPERTS)))
    wr_hi = wr.astype(BF16)
    wr_lo = (wr - wr_hi.astype(F32)).astype(BF16)
    return dict(w_main=w_main, w_f=w_f, b_f=b_f, colgain=colgain,
                wa=w_branch_fox.astype(BF16), wb=w_branch_moba.astype(BF16), wo=w_out.astype(BF16),
                wr2=jnp.concatenate([wr_hi, wr_lo], axis=1))


def _pick(n, pref):
    t = min(n, pref)
    while n % t:
        t //= 2
    return t


def _prompt_layer(x, wts, experts, norm_mix, norm_ffn):
    b, t, _ = x.shape
    n = b * t
    x2 = x.reshape(n, D_MODEL)
    tm = _pick(t, 1024)
    tabs = _rope_tables(jnp.arange(t, dtype=I32))
    proj, logf = _inproj(x2, norm_mix[None, :], wts["w_main"], wts["w_f"], wts["b_f"], wts["colgain"],
                         tabs, tm=tm)
    proj3 = proj.reshape(b, t, PROJ_W)
    c3 = _cumsum(logf.reshape(b, t, LANES), tc=_pick(t, 256))
    ck_t = jnp.transpose(c3[:, :, :N_HEADS], (0, 2, 1))
    oa = _fox_prompt(proj3, c3, ck_t, tq=_pick(t, 512))
    ob = _moba_prompt(proj3)
    x1 = _outproj(oa.reshape(n, Q_W), ob.reshape(n, Q_W), proj, x2, wts["wa"], wts["wb"], wts["wo"],
                  tm=_pick(n, 256))
    y = _moe(x1, norm_ffn[None, :], wts["wr2"], *experts, tm=_pick(n, 256), tb=256)
    return y.reshape(b, t, D_MODEL), proj3, logf.reshape(b, t, LANES)[:, :, :N_HEADS]


def _decode_layer(x, wts, experts, norm_mix, norm_ffn, caches, page_table):
    b, t, _ = x.shape
    n = b * t
    n_pages = page_table.shape[1]
    past = n_pages * PAGE
    assert past % MOBA_BLOCK == 0 and t <= MOBA_BLOCK and (t * N_HEADS) % SUBLANES == 0
    cache_fk, cache_fv, cache_logf, cache_mk, cache_mv = caches
    n_pool = cache_fk.shape[0]
    x2 = x.reshape(n, D_MODEL)
    pos = past + jnp.arange(t, dtype=I32)
    tabs = tuple(jnp.tile(tb_, (b, 1)) for tb_ in _rope_tables(pos))
    proj, logf = _inproj(x2, norm_mix[None, :], wts["w_main"], wts["w_f"], wts["b_f"], wts["colgain"],
                         tabs, tm=n)
    pt_flat = page_table.reshape(-1).astype(I32)
    rows = t * N_HEADS
    logf_bt = logf.reshape(b, t, LANES)[:, :, :N_HEADS]
    logf_new_t = jnp.pad(jnp.transpose(logf_bt, (0, 2, 1)), ((0, 0), (0, 0), (0, LANES - t)))
    nck_rep, c_new_t = _paged_cumsum(pt_flat, jnp.transpose(cache_logf, (0, 2, 1)), logf_new_t,
                                     n_pages=n_pages, pages=_pick(n_pages, 64))
    c_new_t = c_new_t[:, :, :t]
    cq32 = jnp.transpose(c_new_t, (0, 2, 1)).reshape(b, rows, 1)
    r_t = np.arange(rows)[:, None] // N_HEADS
    c_t = np.arange(t * N_KV)[None, :] // N_KV
    new_mask = _head_match_mask(rows, t * N_KV) + np.where(c_t <= r_t, 0.0, NEG).astype(np.float32)
    bias_new = jnp.tile(jnp.repeat(-c_new_t, N_KV, axis=-1), (1, t, 1)) + new_mask[None]
    page_mask = jnp.asarray(_head_match_mask(rows, PAGE_ROWS))
    blk_mask = jnp.asarray(_head_match_mask(rows, MOBA_BLOCK * N_KV))

    def q_rows(col):
        return proj[:, col:col + Q_W].reshape(b, rows, HEAD_DIM)

    def kv_rows(col):
        return proj[:, col:col + KV_W].reshape(b, t * N_KV, HEAD_DIM)

    page_rows = lambda c: c.reshape(n_pool, PAGE_ROWS, HEAD_DIM)
    oa = _fox_decode(pt_flat, q_rows(COL_FOX_Q), cq32, nck_rep, page_mask, kv_rows(COL_FOX_K),
                     kv_rows(COL_FOX_V), bias_new, page_rows(cache_fk), page_rows(cache_fv),
                     n_pages=n_pages, pages=_pick(n_pages, 16))
    ob = _moba_decode(pt_flat, q_rows(COL_MOBA_Q), blk_mask, kv_rows(COL_MOBA_K), kv_rows(COL_MOBA_V),
                      jnp.asarray(new_mask), page_rows(cache_mk), page_rows(cache_mv),
                      n_pages=n_pages, nb=_pick(n_pages * PAGE // MOBA_BLOCK, 8))
    x1 = _outproj(oa.reshape(n, Q_W), ob.reshape(n, Q_W), proj, x2, wts["wa"], wts["wb"], wts["wo"], tm=n)
    y = _moe(x1, norm_ffn[None, :], wts["wr2"], *experts, tm=n, tb=16)
    return y.reshape(b, t, D_MODEL), proj.reshape(b, t, PROJ_W), logf_bt


KV_COLS = (COL_FOX_K, COL_FOX_V, COL_MOBA_K, COL_MOBA_V)


def _kv_pack_body(*refs):
    srcs, dsts = refs[:len(KV_COLS)], refs[len(KV_COLS):]
    tm = srcs[0].shape[0]
    for src, dst in zip(srcs, dsts):
        for hh in range(N_KV):
            dst[pl.ds(hh, tm, stride=N_KV), :] = src[:, hh * HEAD_DIM:(hh + 1) * HEAD_DIM]


def _kv_pack(proj, *, tm):
    n = proj.shape[0]
    return pl.pallas_call(
        _kv_pack_body,
        grid=(n // tm,),
        in_specs=[pl.BlockSpec((tm, KV_W), lambda i, c=c: (i, c // KV_W)) for c in KV_COLS],
        out_specs=[pl.BlockSpec((tm * N_KV, HEAD_DIM), lambda i: (i, 0))] * len(KV_COLS),
        out_shape=[_sds((n * N_KV, HEAD_DIM), F32)] * len(KV_COLS),
        compiler_params=_cparams(("parallel",)),
        name="kv_pack",
    )(*([proj] * len(KV_COLS)))


def _kv_outputs(proj3, logf):
    b, t, _ = proj3.shape
    fk, fv, mk, mv = (a.reshape(1, b, t, N_KV, HEAD_DIM)
                      for a in _kv_pack(proj3.reshape(b * t, PROJ_W), tm=_pick(b * t, 512)))
    return fk, fv, logf[None], mk, mv


def kernel(x_prompt, x_sample, cache_fox_k, cache_fox_v, cache_fox_logf, cache_moba_k, cache_moba_v,
           page_table, norm_mix, w_in, b_forget, fox_q_norm, fox_k_norm, moba_q_norm, moba_k_norm,
           w_branch_fox, w_branch_moba, w_out, norm_ffn, w_router_group, w_router_expert,
           w_exp_gate, w_exp_up, w_exp_down):
    assert w_in.shape[0] == 1, "single-layer trunk"
    wts = _prep_weights(w_in[0], b_forget[0], (fox_q_norm[0], fox_k_norm[0], moba_q_norm[0], moba_k_norm[0]),
                        w_branch_fox[0], w_branch_moba[0], w_out[0], w_router_group[0], w_router_expert[0])
    experts = (w_exp_gate[0], w_exp_up[0], w_exp_down[0])
    caches = (cache_fox_k[0], cache_fox_v[0], cache_fox_logf[0], cache_moba_k[0], cache_moba_v[0])
    ys, proj_s, logf_s = _decode_layer(x_sample, wts, experts, norm_mix[0], norm_ffn[0], caches, page_table)
    yp, proj_p, logf_p = _prompt_layer(x_prompt, wts, experts, norm_mix[0], norm_ffn[0])
    return (yp, ys) + _kv_outputs(proj_p, logf_p) + _kv_outputs(proj_s, logf_s)
```

```python
import functools

import numpy as np
import jax
import jax.numpy as jnp
from jax import lax
from jax.experimental import pallas as pl
from jax.experimental.pallas import tpu as pltpu

F32 = jnp.float32
BF16 = jnp.bfloat16
I32 = jnp.int32
HIGHEST = lax.Precision.HIGHEST

LANES = 128
SUBLANES = 8
D_MODEL = 2048
HEAD_DIM = 128
N_HEADS = 8
N_KV = 4
GROUP = N_HEADS // N_KV
ROT_DIM = HEAD_DIM // 4
ROPE_THETA = 500000.0
ATTN_SCALE = HEAD_DIM ** -0.5
MOBA_BLOCK = 256
MOBA_TOPK = 3
PAGE = 128
PAGE_ROWS = PAGE * N_KV
N_GROUPS = 4
EXPERTS_PER_GROUP = 8
N_EXPERTS = N_GROUPS * EXPERTS_PER_GROUP
D_EXPERT = 512
RMS_EPS = 1e-6
NEG = -1e30
Q_W = N_HEADS * HEAD_DIM
KV_W = N_KV * HEAD_DIM
PROJ_W = 2 * (Q_W + 2 * KV_W) + 2 * D_MODEL
PROJ_TILE = 512
COL_FOX_Q, COL_FOX_K, COL_FOX_V = 0, Q_W, Q_W + KV_W
COL_MOBA_Q = Q_W + 2 * KV_W
COL_MOBA_K, COL_MOBA_V = COL_MOBA_Q + Q_W, COL_MOBA_Q + Q_W + KV_W
COL_GA = 2 * (Q_W + 2 * KV_W)
COL_GB = COL_GA + D_MODEL
VMEM_LIMIT = 56 * 1024 * 1024


def _cparams(sem):
    return pltpu.CompilerParams(dimension_semantics=sem, vmem_limit_bytes=VMEM_LIMIT)


def _sds(shape, dtype):
    return jax.ShapeDtypeStruct(shape, dtype)


def _head_rmsnorm(y, gain):
    ms = jnp.mean(y * y, axis=-1, keepdims=True)
    return y * lax.rsqrt(ms + RMS_EPS) * gain


def _inproj_body(x_ref, g_ref, w_ref, wf_ref, bf_ref, cg_ref, cos_ref, sa_ref, sb_ref,
                 proj_ref, logf_ref, h_scr):
    j = pl.program_id(1)

    @pl.when(j == 0)
    def _():
        x = x_ref[...]
        ms = jnp.mean(x * x, axis=-1, keepdims=True)
        h = (x * lax.rsqrt(ms + RMS_EPS) * g_ref[...]).astype(BF16)
        h_scr[...] = h
        z = jnp.dot(h, wf_ref[...], preferred_element_type=F32) + bf_ref[...]
        logf_ref[...] = -(jnp.maximum(-z, 0.0) + jnp.log1p(jnp.exp(-jnp.abs(z))))

    fox_normed = (j >= COL_FOX_Q // PROJ_TILE) & (j < COL_FOX_V // PROJ_TILE)
    moba_normed = (j >= COL_MOBA_Q // PROJ_TILE) & (j < COL_MOBA_V // PROJ_TILE)

    acc = jnp.dot(h_scr[...], w_ref[...], preferred_element_type=F32)

    @pl.when(jnp.logical_not(fox_normed | moba_normed))
    def _():
        proj_ref[...] = acc

    @pl.when(fox_normed)
    def _():
        for hh in range(PROJ_TILE // HEAD_DIM):
            sl = slice(hh * HEAD_DIM, (hh + 1) * HEAD_DIM)
            proj_ref[:, sl] = _head_rmsnorm(acc[:, sl], cg_ref[:, sl])

    @pl.when(moba_normed)
    def _():
        cos, sa, sb = cos_ref[...], sa_ref[...], sb_ref[...]
        half = ROT_DIM // 2
        for hh in range(PROJ_TILE // HEAD_DIM):
            sl = slice(hh * HEAD_DIM, (hh + 1) * HEAD_DIM)
            y = _head_rmsnorm(acc[:, sl], cg_ref[:, sl])
            proj_ref[:, sl] = (y * cos + pltpu.roll(y, half, axis=1) * sa
                               + pltpu.roll(y, HEAD_DIM - half, axis=1) * sb)


def _inproj(x2, g, w_main, w_f, b_f, colgain, tabs, *, tm):
    n = x2.shape[0]
    ntab = tabs[0].shape[0] // tm
    tab_spec = pl.BlockSpec((tm, LANES), lambda i, j: (i % ntab, 0))
    return pl.pallas_call(
        _inproj_body,
        grid=(n // tm, PROJ_W // PROJ_TILE),
        in_specs=[
            pl.BlockSpec((tm, D_MODEL), lambda i, j: (i, 0)),
            pl.BlockSpec((1, D_MODEL), lambda i, j: (0, 0)),
            pl.BlockSpec((D_MODEL, PROJ_TILE), lambda i, j: (0, j)),
            pl.BlockSpec((D_MODEL, LANES), lambda i, j: (0, 0)),
            pl.BlockSpec((1, LANES), lambda i, j: (0, 0)),
            pl.BlockSpec((1, PROJ_TILE), lambda i, j: (0, j)),
            tab_spec, tab_spec, tab_spec,
        ],
        out_specs=[pl.BlockSpec((tm, PROJ_TILE), lambda i, j: (i, j)),
                   pl.BlockSpec((tm, LANES), lambda i, j: (i, 0))],
        out_shape=[_sds((n, PROJ_W), F32), _sds((n, LANES), F32)],
        scratch_shapes=[pltpu.VMEM((tm, D_MODEL), BF16)],
        compiler_params=_cparams(("parallel", "arbitrary")),
        name="inproj",
    )(x2, g, w_main, w_f, b_f, colgain, *tabs)


def _lower_tri(n, strict=False):
    r = lax.broadcasted_iota(I32, (n, n), 0)
    c = lax.broadcasted_iota(I32, (n, n), 1)
    return (c < r) if strict else (c <= r)


def _cumsum_body(x_ref, o_ref, carry):
    @pl.when(pl.program_id(1) == 0)
    def _():
        carry[...] = jnp.zeros_like(carry)

    x = x_ref[0]
    tc = x.shape[0]
    tri = _lower_tri(tc).astype(F32)
    cs = jnp.dot(tri, x, precision=HIGHEST, preferred_element_type=F32) + carry[...]
    o_ref[0] = cs
    carry[...] = cs[tc - 1:tc, :]


def _cumsum(logf3, *, tc):
    b, t, w = logf3.shape
    return pl.pallas_call(
        _cumsum_body,
        grid=(b, t // tc),
        in_specs=[pl.BlockSpec((1, tc, w), lambda i, j: (i, j, 0))],
        out_specs=pl.BlockSpec((1, tc, w), lambda i, j: (i, j, 0)),
        out_shape=_sds((b, t, w), F32),
        scratch_shapes=[pltpu.VMEM((1, w), F32)],
        compiler_params=_cparams(("parallel", "arbitrary")),
        name="logf_cumsum",
    )(logf3)


def _dot_exact01(x, m01):
    hi = x.astype(BF16)
    r1 = x - hi.astype(F32)
    mid = r1.astype(BF16)
    lo = (r1 - mid.astype(F32)).astype(BF16)
    return (jnp.dot(hi, m01, preferred_element_type=F32) + jnp.dot(mid, m01, preferred_element_type=F32)
            + jnp.dot(lo, m01, preferred_element_type=F32))


def _paged_cumsum_body(pt_ref, *refs, pages):
    del pt_ref
    page_refs = refs[:pages]
    new_ref, cum_rep_ref, cum_ref = refs[pages:pages + 3]
    o_ref, onew_ref, carry = refs[pages + 3:]
    s = pl.program_id(1)

    @pl.when(s == 0)
    def _():
        carry[...] = jnp.zeros_like(carry)

    x = jnp.concatenate([r[0] for r in page_refs], axis=0)
    w = _dot_exact01(x, cum_rep_ref[...]).reshape(pages, N_HEADS, PAGE_ROWS)
    tot = w[:, :, PAGE_ROWS - 1:PAGE_ROWS]
    inc = tot
    sh = 1
    while sh < pages:
        inc = inc + jnp.concatenate([jnp.zeros((sh, N_HEADS, 1), F32), inc[:pages - sh]], axis=0)
        sh *= 2
    offs = jnp.concatenate([jnp.zeros((1, N_HEADS, 1), F32), inc[:pages - 1]], axis=0) + carry[...]
    for p in range(pages):
        o_ref[0, :, p * PAGE_ROWS:(p + 1) * PAGE_ROWS] = -(w[p] + offs[p])
    c = carry[...] + inc[pages - 1]
    carry[...] = c

    @pl.when(s == pl.num_programs(1) - 1)
    def _():
        onew_ref[0] = _dot_exact01(new_ref[0], cum_ref[...]) + c


def _paged_cumsum(pt_flat, cache_logf_t, logf_new_t, *, n_pages, pages):
    b = logf_new_t.shape[0]
    upper = np.triu(np.ones((PAGE, PAGE), np.float32))
    cum = jnp.asarray(upper, BF16)
    cum_rep = jnp.asarray(np.repeat(upper, N_KV, axis=1), BF16)

    def page_map(p):
        return lambda i, s, pt: (pt[i * n_pages + s * pages + p], 0, 0)

    const = lambda i, s, pt: (0, 0)
    grid_spec = pltpu.PrefetchScalarGridSpec(
        num_scalar_prefetch=1,
        grid=(b, n_pages // pages),
        in_specs=[pl.BlockSpec((1, N_HEADS, PAGE), page_map(p)) for p in range(pages)]
        + [pl.BlockSpec((1, N_HEADS, LANES), lambda i, s, pt: (i, 0, 0)),
           pl.BlockSpec((PAGE, PAGE_ROWS), const), pl.BlockSpec((PAGE, PAGE), const)],
        out_specs=[pl.BlockSpec((1, N_HEADS, pages * PAGE_ROWS), lambda i, s, pt: (i, 0, s)),
                   pl.BlockSpec((1, N_HEADS, LANES), lambda i, s, pt: (i, 0, 0))],
        scratch_shapes=[pltpu.VMEM((N_HEADS, 1), F32)],
    )
    return pl.pallas_call(
        functools.partial(_paged_cumsum_body, pages=pages),
        grid_spec=grid_spec,
        out_shape=[_sds((b, N_HEADS, n_pages * PAGE_ROWS), F32), _sds((b, N_HEADS, LANES), F32)],
        compiler_params=_cparams(("parallel", "arbitrary")),
        name="paged_logf_cumsum",
    )(pt_flat, *([cache_logf_t] * pages), logf_new_t, cum_rep, cum)


def _softmax_tile(t, rowshift, v, carry):
    m, l, acc = carry
    m_new = jnp.maximum(m, jnp.max(t, axis=-1, keepdims=True) + rowshift)
    p = jnp.exp(t + (rowshift - m_new))
    alpha = jnp.exp(m - m_new)
    l = alpha * l + jnp.sum(p, axis=-1, keepdims=True)
    acc = alpha * acc + jnp.dot(p.astype(BF16), v, preferred_element_type=F32)
    return m_new, l, acc


def _softmax_init(rows):
    return (jnp.full((rows, 1), NEG, F32), jnp.zeros((rows, 1), F32), jnp.zeros((rows, HEAD_DIM), F32))


def _qk(q, k):
    return lax.dot_general(q, k, (((1,), (1,)), ((), ())), preferred_element_type=F32)


def _fox_body(q_ref, k_ref, v_ref, cq_ref, ck_ref, o_ref, *, tq):
    h = pl.program_id(1)
    qi = pl.program_id(2)
    lane = lax.broadcasted_iota(I32, (tq, LANES), 1)
    qs, cqs = [], []
    for g in range(GROUP):
        qs.append((q_ref[0, :, g * HEAD_DIM:(g + 1) * HEAD_DIM] * ATTN_SCALE).astype(BF16))
        cqs.append(jnp.sum(jnp.where(lane == h * GROUP + g, cq_ref[0], 0.0), axis=-1, keepdims=True))

    def tile(start, diag, carries):
        k = k_ref[0, pl.ds(start, tq), :].astype(BF16)
        v = v_ref[0, pl.ds(start, tq), :].astype(BF16)
        out = []
        for g in range(GROUP):
            ck = ck_ref[0, pl.ds(h * GROUP + g, 1), pl.ds(start, tq)]
            t = _qk(qs[g], k) - ck
            if diag:
                t = jnp.where(_lower_tri(tq), t, NEG)
            out.append(_softmax_tile(t, cqs[g], v, carries[g]))
        return tuple(out)

    carries = tile(pl.multiple_of(qi * tq, tq), True, (_softmax_init(tq),) * GROUP)
    carries = lax.fori_loop(0, qi, lambda j, c: tile(pl.multiple_of(j * tq, tq), False, c), carries)
    for g in range(GROUP):
        _, l, acc = carries[g]
        o_ref[0, :, g * HEAD_DIM:(g + 1) * HEAD_DIM] = (acc / l).astype(o_ref.dtype)


def _fox_prompt(proj3, c3, ck_t, *, tq):
    b, t, _ = proj3.shape
    gw = GROUP * HEAD_DIM
    qb, kb, vb = COL_FOX_Q // gw, COL_FOX_K // HEAD_DIM, COL_FOX_V // HEAD_DIM
    return pl.pallas_call(
        functools.partial(_fox_body, tq=tq),
        grid=(b, N_KV, t // tq),
        in_specs=[
            pl.BlockSpec((1, tq, gw), lambda i, h, q: (i, q, qb + h)),
            pl.BlockSpec((1, t, HEAD_DIM), lambda i, h, q: (i, 0, kb + h)),
            pl.BlockSpec((1, t, HEAD_DIM), lambda i, h, q: (i, 0, vb + h)),
            pl.BlockSpec((1, tq, LANES), lambda i, h, q: (i, q, 0)),
            pl.BlockSpec((1, N_HEADS, t), lambda i, h, q: (i, 0, 0)),
        ],
        out_specs=pl.BlockSpec((1, tq, gw), lambda i, h, q: (i, q, h)),
        out_shape=_sds((b, t, Q_W), BF16),
        compiler_params=_cparams(("parallel", "parallel", "arbitrary")),
        name="fox_prompt",
    )(proj3, proj3, proj3, c3, ck_t)


def _topk_select(g, elig, n_cand, axis):
    idx = lax.broadcasted_iota(I32, g.shape, axis)
    g = jnp.where(elig, g, -jnp.inf)
    cnt = jnp.zeros(g.shape, F32)
    for m in range(n_cand):
        gm = g[m:m + 1, :] if axis == 0 else g[:, m:m + 1]
        beats = (gm > g) | ((gm == g) & (m < idx))
        cnt = cnt + jnp.where(beats, 1.0, 0.0)
    return elig & (cnt < MOBA_TOPK)


def _moba_body(q_ref, qall_ref, k_ref, v_ref, o_ref, kmean_scr, sel_scr, *, n_blk):
    qi = pl.program_id(2)
    bs = MOBA_BLOCK
    tq = 2 * bs
    nb8 = kmean_scr.shape[0]
    t_all = qall_ref.shape[1]

    @pl.when(qi == 0)
    def _():
        kmean_scr[...] = jnp.zeros_like(kmean_scr)
        for n in range(n_blk):
            kb = k_ref[0, n * bs:(n + 1) * bs, :]
            kmean_scr[n:n + 1, :] = jnp.mean(kb, axis=0, keepdims=True)
        eye = jnp.where(lax.broadcasted_iota(I32, (bs, bs), 0) == lax.broadcasted_iota(I32, (bs, bs), 1),
                        1.0, 0.0).astype(BF16)
        blk = lax.broadcasted_iota(I32, (nb8, t_all), 0)
        q_blk = lax.shift_right_logical(lax.broadcasted_iota(I32, (nb8, t_all), 1), bs.bit_length() - 1)
        for g in range(GROUP):
            gate_t = lax.dot_general(kmean_scr[...], qall_ref[0, :, g * HEAD_DIM:(g + 1) * HEAD_DIM],
                                     (((1,), (1,)), ((), ())), precision=HIGHEST, preferred_element_type=F32)
            sel_t = jnp.where(_topk_select(gate_t, blk < q_blk, n_blk, 0), 1.0, 0.0).astype(BF16)
            for c in range(t_all // bs):
                sel_scr[g, c * bs:(c + 1) * bs, :] = _qk(eye, sel_t[:, c * bs:(c + 1) * bs])

    lane_blk = lax.broadcasted_iota(I32, (tq, nb8), 1)
    qs, sels = [], []
    for g in range(GROUP):
        qs.append((q_ref[0, :, g * HEAD_DIM:(g + 1) * HEAD_DIM] * ATTN_SCALE).astype(BF16))
        sels.append(sel_scr[g, pl.ds(pl.multiple_of(qi * tq, tq), tq), :])

    def load(j):
        start = pl.multiple_of(j * tq, tq)
        return (k_ref[0, pl.ds(start, tq), :].astype(BF16), v_ref[0, pl.ds(start, tq), :].astype(BF16))

    def row_bias(g, n):
        flag = jnp.sum(jnp.where(lane_blk == n, sels[g], 0.0), axis=-1, keepdims=True)
        return (flag - 1.0) * (-NEG)

    col_first = lax.broadcasted_iota(I32, (tq, tq), 1) < bs
    row_second = lax.broadcasted_iota(I32, (tq, tq), 0) >= bs

    def past_pair(j, carries):
        k, v = load(j)
        out = []
        for g in range(GROUP):
            bias = jnp.where(col_first, row_bias(g, 2 * j), row_bias(g, 2 * j + 1))
            out.append(_softmax_tile(_qk(qs[g], k) + bias, 0.0, v, carries[g]))
        return tuple(out)

    k, v = load(qi)
    carries = []
    for g in range(GROUP):
        t = jnp.where(_lower_tri(tq), _qk(qs[g], k), NEG)
        t = t + jnp.where(col_first & row_second, row_bias(g, 2 * qi), 0.0)
        carries.append(_softmax_tile(t, 0.0, v, _softmax_init(tq)))
    carries = lax.fori_loop(0, qi, past_pair, tuple(carries))
    for g in range(GROUP):
        _, l, acc = carries[g]
        o_ref[0, :, g * HEAD_DIM:(g + 1) * HEAD_DIM] = (acc / l).astype(o_ref.dtype)


def _moba_prompt(proj3):
    b, t, _ = proj3.shape
    n_blk = t // MOBA_BLOCK
    assert t % (2 * MOBA_BLOCK) == 0
    nb8 = -(-n_blk // SUBLANES) * SUBLANES
    gw = GROUP * HEAD_DIM
    qb, kb, vb = COL_MOBA_Q // gw, COL_MOBA_K // HEAD_DIM, COL_MOBA_V // HEAD_DIM
    tq = 2 * MOBA_BLOCK
    return pl.pallas_call(
        functools.partial(_moba_body, n_blk=n_blk),
        grid=(b, N_KV, t // tq),
        in_specs=[
            pl.BlockSpec((1, tq, gw), lambda i, h, q: (i, q, qb + h)),
            pl.BlockSpec((1, t, gw), lambda i, h, q: (i, 0, qb + h)),
            pl.BlockSpec((1, t, HEAD_DIM), lambda i, h, q: (i, 0, kb + h)),
            pl.BlockSpec((1, t, HEAD_DIM), lambda i, h, q: (i, 0, vb + h)),
        ],
        out_specs=pl.BlockSpec((1, tq, gw), lambda i, h, q: (i, q, h)),
        out_shape=_sds((b, t, Q_W), BF16),
        scratch_shapes=[pltpu.VMEM((nb8, HEAD_DIM), F32), pltpu.VMEM((GROUP, t, nb8), F32)],
        compiler_params=_cparams(("parallel", "parallel", "arbitrary")),
        name="moba_prompt",
    )(proj3, proj3, proj3, proj3)


def _fox_decode_body(pt_ref, *refs, pages, rows):
    del pt_ref
    q_ref, cq_ref, nck_ref, mask_ref, knew_ref, vnew_ref, bnew_ref = refs[:7]
    k_refs = refs[7:7 + pages]
    v_refs = refs[7 + pages:7 + 2 * pages]
    o_ref, m_scr, l_scr, acc_scr = refs[7 + 2 * pages:]
    s = pl.program_id(1)

    @pl.when(s == 0)
    def _():
        m_scr[...] = jnp.full_like(m_scr, NEG)
        l_scr[...] = jnp.zeros_like(l_scr)
        acc_scr[...] = jnp.zeros_like(acc_scr)

    q = (q_ref[0] * ATTN_SCALE).astype(BF16)
    cq = cq_ref[0]
    m, l, acc = m_scr[...], l_scr[...], acc_scr[...]
    t = jnp.concatenate([_qk(q, k_refs[p][0].astype(BF16)) for p in range(pages)], axis=1)
    t = t + jnp.concatenate([nck_ref[0]] * (rows // N_HEADS), axis=0) + mask_ref[...]
    m_new = jnp.maximum(m, jnp.max(t, axis=-1, keepdims=True) + cq)
    pf = jnp.exp(t + (cq - m_new))
    alpha = jnp.exp(m - m_new)
    l = alpha * l + jnp.sum(pf, axis=-1, keepdims=True)
    p_ = pf.astype(BF16)
    pv = jnp.dot(p_[:, 0:PAGE_ROWS], v_refs[0][0].astype(BF16), preferred_element_type=F32)
    for p in range(1, pages):
        pv = pv + jnp.dot(p_[:, p * PAGE_ROWS:(p + 1) * PAGE_ROWS], v_refs[p][0].astype(BF16),
                          preferred_element_type=F32)
    acc = alpha * acc + pv
    m_scr[...], l_scr[...], acc_scr[...] = m_new, l, acc

    @pl.when(s == pl.num_programs(1) - 1)
    def _():
        tn = _qk(q, knew_ref[0].astype(BF16)) + bnew_ref[0]
        _, l2, acc2 = _softmax_tile(tn, cq, vnew_ref[0].astype(BF16), (m_new, l, acc))
        o_ref[0] = (acc2 / l2).astype(o_ref.dtype)


def _fox_decode(pt_flat, q32, cq32, nck_rep, mask, k_new, v_new, bias_new, cache_k, cache_v,
                *, n_pages, pages):
    b, rows, _ = q32.shape
    n_new = k_new.shape[1]

    def page_map(p):
        return lambda i, s, pt: (pt[i * n_pages + s * pages + p], 0, 0)

    per_b = lambda i, s, pt: (i, 0, 0)
    grid_spec = pltpu.PrefetchScalarGridSpec(
        num_scalar_prefetch=1,
        grid=(b, n_pages // pages),
        in_specs=[
            pl.BlockSpec((1, rows, HEAD_DIM), per_b),
            pl.BlockSpec((1, rows, 1), per_b),
            pl.BlockSpec((1, N_HEADS, pages * PAGE_ROWS), lambda i, s, pt: (i, 0, s)),
            pl.BlockSpec((rows, pages * PAGE_ROWS), lambda i, s, pt: (0, 0)),
            pl.BlockSpec((1, n_new, HEAD_DIM), per_b),
            pl.BlockSpec((1, n_new, HEAD_DIM), per_b),
            pl.BlockSpec((1, rows, n_new), per_b),
        ] + [pl.BlockSpec((1, PAGE_ROWS, HEAD_DIM), page_map(p)) for p in range(pages)] * 2,
        out_specs=pl.BlockSpec((1, rows, HEAD_DIM), per_b),
        scratch_shapes=[pltpu.VMEM((rows, 1), F32), pltpu.VMEM((rows, 1), F32),
                        pltpu.VMEM((rows, HEAD_DIM), F32)],
    )
    return pl.pallas_call(
        functools.partial(_fox_decode_body, pages=pages, rows=rows),
        grid_spec=grid_spec,
        out_shape=_sds((b, rows, HEAD_DIM), BF16),
        compiler_params=_cparams(("parallel", "arbitrary")),
        name="fox_decode",
    )(pt_flat, q32, cq32, nck_rep, jnp.tile(mask, (1, pages)), k_new, v_new, bias_new,
      *([cache_k] * pages), *([cache_v] * pages))


def _tree_sum(x):
    while x.shape[0] > 1:
        half = x.shape[0] // 2
        x = x[:half] + x[half:]
    return x[0]


def _moba_decode_body(pt_ref, *refs, nb, n_past, rows):
    del pt_ref
    ppb = MOBA_BLOCK // PAGE
    pages = nb * ppb
    q_ref, mask_ref, knew_ref, vnew_ref, bnew_ref = refs[:5]
    k_refs = refs[5:5 + pages]
    v_refs = refs[5 + pages:5 + 2 * pages]
    o_ref, m_scr, l_scr, g_scr, acc_scr = refs[5 + 2 * pages:]
    s = pl.program_id(1)

    @pl.when(s == 0)
    def _():
        m_scr[...] = jnp.zeros_like(m_scr)
        l_scr[...] = jnp.zeros_like(l_scr)
        g_scr[...] = jnp.zeros_like(g_scr)

    qf = q_ref[0]
    q = (qf * ATTN_SCALE).astype(BF16)
    lane = lax.broadcasted_iota(I32, (rows, LANES), 1)
    m_all, l_all, g_all = m_scr[...], l_scr[...], g_scr[...]
    for j in range(nb):
        n = s * nb + j
        kf = jnp.concatenate([k_refs[j * ppb + p][0] for p in range(ppb)], axis=0)
        vf = jnp.concatenate([v_refs[j * ppb + p][0] for p in range(ppb)], axis=0)
        t = _qk(q, kf.astype(BF16)) + mask_ref[...]
        m_n = jnp.max(t, axis=-1, keepdims=True)
        p_ = jnp.exp(t - m_n)
        l_n = jnp.sum(p_, axis=-1, keepdims=True)
        acc_scr[n] = jnp.dot(p_.astype(BF16), vf.astype(BF16), preferred_element_type=F32)
        s8 = _tree_sum(kf.reshape(MOBA_BLOCK * N_KV // SUBLANES, SUBLANES, HEAD_DIM))
        ks = s8[0:N_KV] + s8[N_KV:2 * N_KV]
        ks_h = jnp.concatenate([ks[h // GROUP:h // GROUP + 1] for h in range(N_HEADS)], axis=0)
        ks_r = jnp.concatenate([ks_h] * (rows // N_HEADS), axis=0)
        g_n = jnp.sum(qf * ks_r, axis=-1, keepdims=True) * (1.0 / MOBA_BLOCK)
        m_all = jnp.where(lane == n, m_n, m_all)
        l_all = jnp.where(lane == n, l_n, l_all)
        g_all = jnp.where(lane == n, g_n, g_all)
    m_scr[...], l_scr[...], g_scr[...] = m_all, l_all, g_all

    @pl.when(s == pl.num_programs(1) - 1)
    def _():
        t = _qk(q, knew_ref[0].astype(BF16)) + bnew_ref[...]
        m_own = jnp.max(t, axis=-1, keepdims=True)
        p_own = jnp.exp(t - m_own)
        l_own = jnp.sum(p_own, axis=-1, keepdims=True)
        acc_own = jnp.dot(p_own.astype(BF16), vnew_ref[0].astype(BF16), preferred_element_type=F32)
        sel = _topk_select(g_all, lane < n_past, n_past, 1)
        m_x = jnp.maximum(m_own, jnp.max(jnp.where(sel, m_all, NEG), axis=-1, keepdims=True))
        w = jnp.where(sel, jnp.exp(m_all - m_x), 0.0)
        w_own = jnp.exp(m_own - m_x)
        den = w_own * l_own + jnp.sum(w * l_all, axis=-1, keepdims=True)
        num = w_own * acc_own
        for n in range(n_past):
            num = num + w[:, n:n + 1] * acc_scr[n]
        o_ref[0] = (num / den).astype(o_ref.dtype)


def _moba_decode(pt_flat, q32, mask, k_new, v_new, bias_new, cache_k, cache_v, *, n_pages, nb):
    b, rows, _ = q32.shape
    ppb = MOBA_BLOCK // PAGE
    pages = nb * ppb
    n_past = n_pages // ppb
    assert n_past <= LANES and n_past % nb == 0
    n_new = k_new.shape[1]

    def page_map(p):
        return lambda i, s, pt: (pt[i * n_pages + s * pages + p], 0, 0)

    per_b = lambda i, s, pt: (i, 0, 0)
    grid_spec = pltpu.PrefetchScalarGridSpec(
        num_scalar_prefetch=1,
        grid=(b, n_past // nb),
        in_specs=[
            pl.BlockSpec((1, rows, HEAD_DIM), per_b),
            pl.BlockSpec((rows, ppb * PAGE_ROWS), lambda i, s, pt: (0, 0)),
            pl.BlockSpec((1, n_new, HEAD_DIM), per_b),
            pl.BlockSpec((1, n_new, HEAD_DIM), per_b),
            pl.BlockSpec((rows, n_new), lambda i, s, pt: (0, 0)),
        ] + [pl.BlockSpec((1, PAGE_ROWS, HEAD_DIM), page_map(p)) for p in range(pages)] * 2,
        out_specs=pl.BlockSpec((1, rows, HEAD_DIM), per_b),
        scratch_shapes=[pltpu.VMEM((rows, LANES), F32), pltpu.VMEM((rows, LANES), F32),
                        pltpu.VMEM((rows, LANES), F32), pltpu.VMEM((n_past, rows, HEAD_DIM), F32)],
    )
    return pl.pallas_call(
        functools.partial(_moba_decode_body, nb=nb, n_past=n_past, rows=rows),
        grid_spec=grid_spec,
        out_shape=_sds((b, rows, HEAD_DIM), BF16),
        compiler_params=_cparams(("parallel", "arbitrary")),
        name="moba_decode",
    )(pt_flat, q32, mask, k_new, v_new, bias_new, *([cache_k] * pages), *([cache_v] * pages))


def _outproj_body(oa_ref, ob_ref, ga_ref, gb_ref, x_ref, wa_ref, wb_ref, wo_ref, o_ref):
    ya = jnp.dot(oa_ref[...], wa_ref[...], preferred_element_type=F32)
    yb = jnp.dot(ob_ref[...], wb_ref[...], preferred_element_type=F32)
    mixed = jax.nn.sigmoid(ga_ref[...]) * ya + jax.nn.sigmoid(gb_ref[...]) * yb
    o_ref[...] = x_ref[...] + jnp.dot(mixed.astype(BF16), wo_ref[...], preferred_element_type=F32)


def _outproj(oa, ob, proj, x2, wa, wb, wo, *, tm):
    n = x2.shape[0]
    const = lambda i: (0, 0)
    once = pl.Buffered(1)
    return pl.pallas_call(
        _outproj_body,
        grid=(n // tm,),
        in_specs=[
            pl.BlockSpec((tm, Q_W), lambda i: (i, 0)),
            pl.BlockSpec((tm, Q_W), lambda i: (i, 0)),
            pl.BlockSpec((tm, D_MODEL), lambda i: (i, COL_GA // D_MODEL)),
            pl.BlockSpec((tm, D_MODEL), lambda i: (i, COL_GB // D_MODEL)),
            pl.BlockSpec((tm, D_MODEL), lambda i: (i, 0)),
            pl.BlockSpec((Q_W, D_MODEL), const, pipeline_mode=once),
            pl.BlockSpec((Q_W, D_MODEL), const, pipeline_mode=once),
            pl.BlockSpec((D_MODEL, D_MODEL), const, pipeline_mode=once),
        ],
        out_specs=pl.BlockSpec((tm, D_MODEL), lambda i: (i, 0)),
        out_shape=_sds((n, D_MODEL), F32),
        compiler_params=_cparams(("parallel",)),
        name="outproj",
    )(oa, ob, proj, proj, x2, wa, wb, wo)


def _first_lane(hit, lane):
    return jnp.min(jnp.where(hit, lane.astype(F32), float(LANES)), axis=-1, keepdims=True).astype(I32)


def _router_body(x_ref, g_ref, wr_ref, hn_ref, ii_ref, iw_ref, cnt_ref, cnt_scr):
    @pl.when(pl.program_id(0) == 0)
    def _():
        cnt_scr[...] = jnp.zeros_like(cnt_scr)

    x = x_ref[...]
    tm = x.shape[0]
    ms = jnp.mean(x * x, axis=-1, keepdims=True)
    hn = x * lax.rsqrt(ms + RMS_EPS) * g_ref[...]
    hn_ref[...] = hn
    h_hi = hn.astype(BF16)
    h_lo = (hn - h_hi.astype(F32)).astype(BF16)
    quad = jnp.dot(jnp.concatenate([h_hi, h_lo], axis=0), wr_ref[...], preferred_element_type=F32)
    logits = (quad[:tm, :LANES] + quad[:tm, LANES:]) + (quad[tm:, :LANES] + quad[tm:, LANES:])
    lane = lax.broadcasted_iota(I32, (tm, LANES), 1)
    is_g = lane < N_GROUPS
    gl = jnp.where(is_g, logits, -jnp.inf)
    gmax = jnp.max(gl, axis=-1, keepdims=True)
    grp = _first_lane(gl == gmax, lane)
    p_grp = 1.0 / jnp.sum(jnp.where(is_g, jnp.exp(gl - gmax), 0.0), axis=-1, keepdims=True)
    e_lane = lane - N_GROUPS
    shift = EXPERTS_PER_GROUP.bit_length() - 1
    in_grp = (e_lane < N_EXPERTS) & (lax.shift_right_arithmetic(e_lane, shift) == grp)
    el = jnp.where(in_grp, logits, -jnp.inf)
    v1 = jnp.max(el, axis=-1, keepdims=True)
    i1 = _first_lane(el == v1, lane)
    el2 = jnp.where(lane == i1, -jnp.inf, el)
    v2 = jnp.max(el2, axis=-1, keepdims=True)
    i2 = _first_lane(el2 == v2, lane)
    r = jnp.exp(v2 - v1)
    w1 = p_grp / (1.0 + r)
    w2 = p_grp * r / (1.0 + r)
    e1 = i1 - N_GROUPS
    e2 = i2 - N_GROUPS
    oh = jnp.where((lane == e1) | (lane == e2), 1.0, 0.0)
    before = cnt_scr[...] + jnp.dot(_lower_tri(tm, strict=True).astype(BF16), oh.astype(BF16),
                                    preferred_element_type=F32)
    r1 = jnp.sum(jnp.where(lane == e1, before, 0.0), axis=-1, keepdims=True)
    r2 = jnp.sum(jnp.where(lane == e2, before, 0.0), axis=-1, keepdims=True)
    cnt_scr[...] = cnt_scr[...] + jnp.sum(oh, axis=0, keepdims=True)
    cnt_ref[...] = cnt_scr[...].astype(I32)
    ii = jnp.where(lane == 0, e1, jnp.where(lane == 1, e2, jnp.where(
        lane == 2, r1.astype(I32), jnp.where(lane == 3, r2.astype(I32), 0))))
    ii_ref[...] = ii
    iw_ref[...] = jnp.where(lane == 0, w1, jnp.where(lane == 1, w2, 0.0))


def _router(x1, g, wr2, *, tm):
    n = x1.shape[0]
    return pl.pallas_call(
        _router_body,
        grid=(n // tm,),
        in_specs=[pl.BlockSpec((tm, D_MODEL), lambda i: (i, 0)),
                  pl.BlockSpec((1, D_MODEL), lambda i: (0, 0)),
                  pl.BlockSpec((D_MODEL, 2 * LANES), lambda i: (0, 0))],
        out_specs=[pl.BlockSpec((tm, D_MODEL), lambda i: (i, 0)),
                   pl.BlockSpec((tm, LANES), lambda i: (i, 0)),
                   pl.BlockSpec((tm, LANES), lambda i: (i, 0)),
                   pl.BlockSpec((1, LANES), lambda i: (0, 0))],
        out_shape=[_sds((n, D_MODEL), F32), _sds((n, LANES), I32), _sds((n, LANES), F32),
                   _sds((1, LANES), I32)],
        scratch_shapes=[pltpu.VMEM((1, LANES), F32)],
        compiler_params=_cparams(("arbitrary",)),
        name="moe_router",
    )(x1, g, wr2)


def _row_token_body(dest_ref, tok_ref, *, n_tok, n_rows):
    def zero(i, c):
        tok_ref[i] = 0
        return c

    lax.fori_loop(0, n_rows, zero, 0, unroll=8)

    def put(t, c):
        tok_ref[dest_ref[t]] = t
        tok_ref[dest_ref[n_tok + t]] = t
        return c

    lax.fori_loop(0, n_tok, put, 0, unroll=8)


def _row_token(dest_flat, *, n_rows):
    n_tok = dest_flat.shape[0] // 2
    return pl.pallas_call(
        functools.partial(_row_token_body, n_tok=n_tok, n_rows=n_rows),
        in_specs=[pl.BlockSpec(memory_space=pltpu.SMEM)],
        out_specs=pl.BlockSpec(memory_space=pltpu.SMEM),
        out_shape=_sds((n_rows,), I32),
        name="moe_row_token",
    )(dest_flat)


GATHER_UNROLL = 8


def _row_gather_start(src_hbm, idx_ref, base, buf, row0, sem, n):
    def body(r, c):
        pltpu.make_async_copy(src_hbm.at[pl.ds(idx_ref[base + r], 1)], buf.at[pl.ds(row0 + r, 1)],
                              sem).start()
        return c

    lax.fori_loop(0, n, body, 0, unroll=GATHER_UNROLL)


def _row_gather_wait(src_hbm, buf, sem):
    pltpu.make_async_copy(src_hbm.at[pl.ds(0, buf.shape[0])], buf, sem).wait()


def _experts_body(be_ref, nu_ref, tok_ref, hn_hbm, wg_ref, wu_ref, wd_ref, y_ref,
                  xbuf, wg_s, wu_s, wd_s, sems, *, tb):
    i = pl.program_id(0)
    n_used = nu_ref[0]
    slot = i % 2

    @pl.when(i == 0)
    def _():
        _row_gather_start(hn_hbm, tok_ref, 0, xbuf.at[0], 0, sems.at[0], tb)

    @pl.when(i + 1 < n_used)
    def _():
        _row_gather_start(hn_hbm, tok_ref, (i + 1) * tb, xbuf.at[1 - slot], 0, sems.at[1 - slot], tb)

    new_expert = (i == 0) | (be_ref[i] != be_ref[jnp.maximum(i - 1, 0)])

    @pl.when((i < n_used) & new_expert)
    def _():
        wg_s[...] = wg_ref[0].astype(BF16)
        wu_s[...] = wu_ref[0].astype(BF16)
        wd_s[...] = wd_ref[0].astype(BF16)

    @pl.when(i < n_used)
    def _():
        _row_gather_wait(hn_hbm, xbuf.at[slot], sems.at[slot])
        x = xbuf[slot].astype(BF16)
        hg = jnp.dot(x, wg_s[...], preferred_element_type=F32)
        hu = jnp.dot(x, wu_s[...], preferred_element_type=F32)
        hid = (hg * jax.nn.sigmoid(hg) * hu).astype(BF16)
        y_ref[...] = jnp.dot(hid, wd_s[...], preferred_element_type=F32)

    @pl.when(i >= n_used)
    def _():
        y_ref[...] = jnp.zeros_like(y_ref)


def _experts(blk_expert, n_used, row_tok, hn, wg, wu, wd, *, tb):
    n_blocks = blk_expert.shape[0]
    wmap = lambda i, be, nu, tok: (be[i], 0, 0)
    grid_spec = pltpu.PrefetchScalarGridSpec(
        num_scalar_prefetch=3,
        grid=(n_blocks,),
        in_specs=[pl.BlockSpec(memory_space=pl.ANY),
                  pl.BlockSpec((1, D_MODEL, D_EXPERT), wmap),
                  pl.BlockSpec((1, D_MODEL, D_EXPERT), wmap),
                  pl.BlockSpec((1, D_EXPERT, D_MODEL), wmap)],
        out_specs=pl.BlockSpec((tb, D_MODEL), lambda i, be, nu, tok: (i, 0)),
        scratch_shapes=[pltpu.VMEM((2, tb, D_MODEL), F32),
                        pltpu.VMEM((D_MODEL, D_EXPERT), BF16), pltpu.VMEM((D_MODEL, D_EXPERT), BF16),
                        pltpu.VMEM((D_EXPERT, D_MODEL), BF16), pltpu.SemaphoreType.DMA((2,))],
    )
    return pl.pallas_call(
        functools.partial(_experts_body, tb=tb),
        grid_spec=grid_spec,
        out_shape=_sds((n_blocks * tb, D_MODEL), F32),
        compiler_params=_cparams(("arbitrary",)),
        name="moe_experts",
    )(blk_expert, n_used, row_tok, hn, wg, wu, wd)


def _combine_body(dest_ref, x_ref, iw_ref, y_hbm, o_ref, ybuf, sems, *, tm, n):
    i = pl.program_id(0)
    n_steps = pl.num_programs(0)
    slot = i % 2

    def start(step, to_slot):
        for k in range(2):
            _row_gather_start(y_hbm, dest_ref, k * n + step * tm, ybuf.at[to_slot], k * tm,
                              sems.at[to_slot], tm)

    @pl.when(i == 0)
    def _():
        start(0, 0)

    @pl.when(i + 1 < n_steps)
    def _():
        start(i + 1, 1 - slot)

    _row_gather_wait(y_hbm, ybuf.at[slot], sems.at[slot])
    iw = iw_ref[...]
    o_ref[...] = (x_ref[...] + iw[:, 0:1] * ybuf[slot, 0:tm, :] + iw[:, 1:2] * ybuf[slot, tm:2 * tm, :])


def _combine(dest_flat, x1, iw, y, *, tm):
    n = x1.shape[0]
    grid_spec = pltpu.PrefetchScalarGridSpec(
        num_scalar_prefetch=1,
        grid=(n // tm,),
        in_specs=[pl.BlockSpec((tm, D_MODEL), lambda i, d: (i, 0)),
                  pl.BlockSpec((tm, LANES), lambda i, d: (i, 0)),
                  pl.BlockSpec(memory_space=pl.ANY)],
        out_specs=pl.BlockSpec((tm, D_MODEL), lambda i, d: (i, 0)),
        scratch_shapes=[pltpu.VMEM((2, 2 * tm, D_MODEL), F32), pltpu.SemaphoreType.DMA((2,))],
    )
    return pl.pallas_call(
        functools.partial(_combine_body, tm=tm, n=n),
        grid_spec=grid_spec,
        out_shape=_sds((n, D_MODEL), F32),
        compiler_params=_cparams(("arbitrary",)),
        name="moe_combine",
    )(dest_flat, x1, iw, y)


def _moe(x1, norm_g, wr2, wg, wu, wd, *, tm, tb):
    n = x1.shape[0]
    hn, ii, iw, cnt = _router(x1, norm_g, wr2, tm=tm)
    counts = cnt[0, :N_EXPERTS]
    padded = (counts + tb - 1) // tb * tb
    pend = jnp.cumsum(padded)
    pstart = pend - padded
    n_blocks = -(-2 * n // tb) + N_EXPERTS
    eids = jnp.arange(N_EXPERTS, dtype=I32)
    e12 = ii[:, 0:2]
    start12 = jnp.sum(jnp.where(e12[:, :, None] == eids, pstart, 0), axis=-1)
    dest_flat = (start12 + ii[:, 2:4]).T.reshape(-1).astype(I32)
    blk_start = jnp.arange(n_blocks, dtype=I32) * tb
    blk_expert = jnp.minimum(jnp.sum(blk_start[:, None] >= pend[None, :], axis=-1), N_EXPERTS - 1).astype(I32)
    n_used = (pend[-1:] // tb).astype(I32)
    row_tok = _row_token(dest_flat, n_rows=n_blocks * tb)
    y = _experts(blk_expert, n_used, row_tok, hn, wg, wu, wd, tb=tb)
    return _combine(dest_flat, x1, iw, y, tm=tm)


def _rope_tables(pos):
    half = ROT_DIM // 2
    inv = jnp.power(ROPE_THETA, -jnp.arange(0, ROT_DIM, 2, dtype=F32) / ROT_DIM)
    ang = pos.astype(F32)[:, None] * inv[None, :]
    cos, sin = jnp.cos(ang), jnp.sin(ang)
    t = pos.shape[0]
    rest = HEAD_DIM - ROT_DIM
    cos_t = jnp.concatenate([cos, cos, jnp.ones((t, rest), F32)], axis=1)
    sa = jnp.concatenate([jnp.zeros((t, half), F32), sin, jnp.zeros((t, rest), F32)], axis=1)
    sb = jnp.concatenate([-sin, jnp.zeros((t, half + rest), F32)], axis=1)
    return cos_t, sa, sb


def _head_match_mask(rows, cols):
    r = np.arange(rows)[:, None]
    c = np.arange(cols)[None, :]
    return np.where((c % N_KV) == ((r % N_HEADS) // GROUP), 0.0, NEG).astype(np.float32)


def _prep_weights(w_in, b_forget, gains, w_branch_fox, w_branch_moba, w_out, w_router_group,
                  w_router_expert):
    fq, fk, mq, mk = gains
    off_f = Q_W + 2 * KV_W
    w_main = jnp.concatenate([w_in[:, :off_f], w_in[:, off_f + N_HEADS:]], axis=1).astype(BF16)
    w_f = jnp.pad(w_in[:, off_f:off_f + N_HEADS], ((0, 0), (0, LANES - N_HEADS))).astype(BF16)
    b_f = jnp.pad(b_forget, (0, LANES - N_HEADS))[None, :]
    ones = lambda w: jnp.ones((w,), F32)
    colgain = jnp.concatenate([
        jnp.tile(fq, N_HEADS), jnp.tile(fk, N_KV), ones(KV_W),
        jnp.tile(mq, N_HEADS), jnp.tile(mk, N_KV), ones(KV_W), ones(2 * D_MODEL)])[None, :]
    wr = jnp.pad(jnp.concatenate([w_router_group, w_router_expert], axis=1),
                 ((0, 0), (0, LANES - N_GROUPS - N_EXPERTS)))
    wr_hi = wr.astype(BF16)
    wr_lo = (wr - wr_hi.astype(F32)).astype(BF16)
    return dict(w_main=w_main, w_f=w_f, b_f=b_f, colgain=colgain,
                wa=w_branch_fox.astype(BF16), wb=w_branch_moba.astype(BF16), wo=w_out.astype(BF16),
                wr2=jnp.concatenate([wr_hi, wr_lo], axis=1))


def _pick(n, pref):
    t = min(n, pref)
    while n % t:
        t //= 2
    return t


def _prompt_layer(x, wts, experts, norm_mix, norm_ffn):
    b, t, _ = x.shape
    n = b * t
    x2 = x.reshape(n, D_MODEL)
    tm = _pick(t, 1024)
    tabs = _rope_tables(jnp.arange(t, dtype=I32))
    proj, logf = _inproj(x2, norm_mix[None, :], wts["w_main"], wts["w_f"], wts["b_f"], wts["colgain"],
                         tabs, tm=tm)
    proj3 = proj.reshape(b, t, PROJ_W)
    c3 = _cumsum(logf.reshape(b, t, LANES), tc=_pick(t, 256))
    ck_t = jnp.transpose(c3[:, :, :N_HEADS], (0, 2, 1))
    oa = _fox_prompt(proj3, c3, ck_t, tq=_pick(t, 512))
    ob = _moba_prompt(proj3)
    x1 = _outproj(oa.reshape(n, Q_W), ob.reshape(n, Q_W), proj, x2, wts["wa"], wts["wb"], wts["wo"],
                  tm=_pick(n, 256))
    y = _moe(x1, norm_ffn[None, :], wts["wr2"], *experts, tm=_pick(n, 256), tb=256)
    return y.reshape(b, t, D_MODEL), proj3, logf.reshape(b, t, LANES)[:, :, :N_HEADS]


def _decode_layer(x, wts, experts, norm_mix, norm_ffn, caches, page_table):
    b, t, _ = x.shape
    n = b * t
    n_pages = page_table.shape[1]
    past = n_pages * PAGE
    assert past % MOBA_BLOCK == 0 and t <= MOBA_BLOCK and (t * N_HEADS) % SUBLANES == 0
    cache_fk, cache_fv, cache_logf, cache_mk, cache_mv = caches
    n_pool = cache_fk.shape[0]
    x2 = x.reshape(n, D_MODEL)
    pos = past + jnp.arange(t, dtype=I32)
    tabs = tuple(jnp.tile(tb_, (b, 1)) for tb_ in _rope_tables(pos))
    proj, logf = _inproj(x2, norm_mix[None, :], wts["w_main"], wts["w_f"], wts["b_f"], wts["colgain"],
                         tabs, tm=n)
    pt_flat = page_table.reshape(-1).astype(I32)
    rows = t * N_HEADS
    logf_bt = logf.reshape(b, t, LANES)[:, :, :N_HEADS]
    logf_new_t = jnp.pad(jnp.transpose(logf_bt, (0, 2, 1)), ((0, 0), (0, 0), (0, LANES - t)))
    nck_rep, c_new_t = _paged_cumsum(pt_flat, jnp.transpose(cache_logf, (0, 2, 1)), logf_new_t,
                                     n_pages=n_pages, pages=_pick(n_pages, 64))
    c_new_t = c_new_t[:, :, :t]
    cq32 = jnp.transpose(c_new_t, (0, 2, 1)).reshape(b, rows, 1)
    r_t = np.arange(rows)[:, None] // N_HEADS
    c_t = np.arange(t * N_KV)[None, :] // N_KV
    new_mask = _head_match_mask(rows, t * N_KV) + np.where(c_t <= r_t, 0.0, NEG).astype(np.float32)
    bias_new = jnp.tile(jnp.repeat(-c_new_t, N_KV, axis=-1), (1, t, 1)) + new_mask[None]
    page_mask = jnp.asarray(_head_match_mask(rows, PAGE_ROWS))
    blk_mask = jnp.asarray(_head_match_mask(rows, MOBA_BLOCK * N_KV))

    def q_rows(col):
        return proj[:, col:col + Q_W].reshape(b, rows, HEAD_DIM)

    def kv_rows(col):
        return proj[:, col:col + KV_W].reshape(b, t * N_KV, HEAD_DIM)

    page_rows = lambda c: c.reshape(n_pool, PAGE_ROWS, HEAD_DIM)
    oa = _fox_decode(pt_flat, q_rows(COL_FOX_Q), cq32, nck_rep, page_mask, kv_rows(COL_FOX_K),
                     kv_rows(COL_FOX_V), bias_new, page_rows(cache_fk), page_rows(cache_fv),
                     n_pages=n_pages, pages=_pick(n_pages, 16))
    ob = _moba_decode(pt_flat, q_rows(COL_MOBA_Q), blk_mask, kv_rows(COL_MOBA_K), kv_rows(COL_MOBA_V),
                      jnp.asarray(new_mask), page_rows(cache_mk), page_rows(cache_mv),
                      n_pages=n_pages, nb=_pick(n_pages * PAGE // MOBA_BLOCK, 8))
    x1 = _outproj(oa.reshape(n, Q_W), ob.reshape(n, Q_W), proj, x2, wts["wa"], wts["wb"], wts["wo"], tm=n)
    y = _moe(x1, norm_ffn[None, :], wts["wr2"], *experts, tm=n, tb=16)
    return y.reshape(b, t, D_MODEL), proj.reshape(b, t, PROJ_W), logf_bt


KV_COLS = (COL_FOX_K, COL_FOX_V, COL_MOBA_K, COL_MOBA_V)


def _kv_pack_body(*refs):
    srcs, dsts = refs[:len(KV_COLS)], refs[len(KV_COLS):]
    tm = srcs[0].shape[0]
    for src, dst in zip(srcs, dsts):
        for hh in range(N_KV):
            dst[pl.ds(hh, tm, stride=N_KV), :] = src[:, hh * HEAD_DIM:(hh + 1) * HEAD_DIM]


def _kv_pack(proj, *, tm):
    n = proj.shape[0]
    return pl.pallas_call(
        _kv_pack_body,
        grid=(n // tm,),
        in_specs=[pl.BlockSpec((tm, KV_W), lambda i, c=c: (i, c // KV_W)) for c in KV_COLS],
        out_specs=[pl.BlockSpec((tm * N_KV, HEAD_DIM), lambda i: (i, 0))] * len(KV_COLS),
        out_shape=[_sds((n * N_KV, HEAD_DIM), F32)] * len(KV_COLS),
        compiler_params=_cparams(("parallel",)),
        name="kv_pack",
    )(*([proj] * len(KV_COLS)))


def _kv_outputs(proj3, logf):
    b, t, _ = proj3.shape
    fk, fv, mk, mv = (a.reshape(1, b, t, N_KV, HEAD_DIM)
                      for a in _kv_pack(proj3.reshape(b * t, PROJ_W), tm=_pick(b * t, 512)))
    return fk, fv, logf[None], mk, mv


def kernel(x_prompt, x_sample, cache_fox_k, cache_fox_v, cache_fox_logf, cache_moba_k, cache_moba_v,
           page_table, norm_mix, w_in, b_forget, fox_q_norm, fox_k_norm, moba_q_norm, moba_k_norm,
           w_branch_fox, w_branch_moba, w_out, norm_ffn, w_router_group, w_router_expert,
           w_exp_gate, w_exp_up, w_exp_down):
    assert w_in.shape[0] == 1, "single-layer trunk"
    wts = _prep_weights(w_in[0], b_forget[0], (fox_q_norm[0], fox_k_norm[0], moba_q_norm[0], moba_k_norm[0]),
                        w_branch_fox[0], w_branch_moba[0], w_out[0], w_router_group[0], w_router_expert[0])
    experts = (w_exp_gate[0], w_exp_up[0], w_exp_down[0])
    caches = (cache_fox_k[0], cache_fox_v[0], cache_fox_logf[0], cache_moba_k[0], cache_moba_v[0])
    ys, proj_s, logf_s = _decode_layer(x_sample, wts, experts, norm_mix[0], norm_ffn[0], caches, page_table)
    yp, proj_p, logf_p = _prompt_layer(x_prompt, wts, experts, norm_mix[0], norm_ffn[0])
    return (yp, ys) + _kv_outputs(proj_p, logf_p) + _kv_outputs(proj_s, logf_s)
```

```python
import functools

import numpy as np
import jax
import jax.numpy as jnp
from jax import lax
from jax.experimental import pallas as pl
from jax.experimental.pallas import tpu as pltpu

F32 = jnp.float32
BF16 = jnp.bfloat16
I32 = jnp.int32
HIGHEST = lax.Precision.HIGHEST

LANES = 128
SUBLANES = 8
D_MODEL = 2048
HEAD_DIM = 128
N_HEADS = 8
N_KV = 4
GROUP = N_HEADS // N_KV
ROT_DIM = HEAD_DIM // 4
ROPE_THETA = 500000.0
ATTN_SCALE = HEAD_DIM ** -0.5
MOBA_BLOCK = 256
MOBA_TOPK = 3
PAGE = 128
PAGE_ROWS = PAGE * N_KV
N_GROUPS = 4
EXPERTS_PER_GROUP = 8
N_EXPERTS = N_GROUPS * EXPERTS_PER_GROUP
D_EXPERT = 512
RMS_EPS = 1e-6
NEG = -1e30
Q_W = N_HEADS * HEAD_DIM
KV_W = N_KV * HEAD_DIM
PROJ_W = 2 * (Q_W + 2 * KV_W) + 2 * D_MODEL
PROJ_TILE = 512
COL_FOX_Q, COL_FOX_K, COL_FOX_V = 0, Q_W, Q_W + KV_W
COL_MOBA_Q = Q_W + 2 * KV_W
COL_MOBA_K, COL_MOBA_V = COL_MOBA_Q + Q_W, COL_MOBA_Q + Q_W + KV_W
COL_GA = 2 * (Q_W + 2 * KV_W)
COL_GB = COL_GA + D_MODEL
VMEM_LIMIT = 56 * 1024 * 1024


def _cparams(sem):
    return pltpu.CompilerParams(dimension_semantics=sem, vmem_limit_bytes=VMEM_LIMIT)


def _sds(shape, dtype):
    return jax.ShapeDtypeStruct(shape, dtype)


def _head_rmsnorm(y, gain):
    ms = jnp.mean(y * y, axis=-1, keepdims=True)
    return y * lax.rsqrt(ms + RMS_EPS) * gain


def _inproj_body(x_ref, g_ref, w_ref, wf_ref, bf_ref, cg_ref, cos_ref, sa_ref, sb_ref,
                 proj_ref, logf_ref, h_scr):
    j = pl.program_id(1)

    @pl.when(j == 0)
    def _():
        x = x_ref[...]
        ms = jnp.mean(x * x, axis=-1, keepdims=True)
        h = (x * lax.rsqrt(ms + RMS_EPS) * g_ref[...]).astype(BF16)
        h_scr[...] = h
        z = jnp.dot(h, wf_ref[...], preferred_element_type=F32) + bf_ref[...]
        logf_ref[...] = -(jnp.maximum(-z, 0.0) + jnp.log1p(jnp.exp(-jnp.abs(z))))

    fox_normed = (j >= COL_FOX_Q // PROJ_TILE) & (j < COL_FOX_V // PROJ_TILE)
    moba_normed = (j >= COL_MOBA_Q // PROJ_TILE) & (j < COL_MOBA_V // PROJ_TILE)

    acc = jnp.dot(h_scr[...], w_ref[...], preferred_element_type=F32)

    @pl.when(jnp.logical_not(fox_normed | moba_normed))
    def _():
        proj_ref[...] = acc

    @pl.when(fox_normed)
    def _():
        for hh in range(PROJ_TILE // HEAD_DIM):
            sl = slice(hh * HEAD_DIM, (hh + 1) * HEAD_DIM)
            proj_ref[:, sl] = _head_rmsnorm(acc[:, sl], cg_ref[:, sl])

    @pl.when(moba_normed)
    def _():
        cos, sa, sb = cos_ref[...], sa_ref[...], sb_ref[...]
        half = ROT_DIM // 2
        for hh in range(PROJ_TILE // HEAD_DIM):
            sl = slice(hh * HEAD_DIM, (hh + 1) * HEAD_DIM)
            y = _head_rmsnorm(acc[:, sl], cg_ref[:, sl])
            proj_ref[:, sl] = (y * cos + pltpu.roll(y, half, axis=1) * sa
                               + pltpu.roll(y, HEAD_DIM - half, axis=1) * sb)


def _inproj(x2, g, w_main, w_f, b_f, colgain, tabs, *, tm):
    n = x2.shape[0]
    ntab = tabs[0].shape[0] // tm
    tab_spec = pl.BlockSpec((tm, LANES), lambda i, j: (i % ntab, 0))
    return pl.pallas_call(
        _inproj_body,
        grid=(n // tm, PROJ_W // PROJ_TILE),
        in_specs=[
            pl.BlockSpec((tm, D_MODEL), lambda i, j: (i, 0)),
            pl.BlockSpec((1, D_MODEL), lambda i, j: (0, 0)),
            pl.BlockSpec((D_MODEL, PROJ_TILE), lambda i, j: (0, j)),
            pl.BlockSpec((D_MODEL, LANES), lambda i, j: (0, 0)),
            pl.BlockSpec((1, LANES), lambda i, j: (0, 0)),
            pl.BlockSpec((1, PROJ_TILE), lambda i, j: (0, j)),
            tab_spec, tab_spec, tab_spec,
        ],
        out_specs=[pl.BlockSpec((tm, PROJ_TILE), lambda i, j: (i, j)),
                   pl.BlockSpec((tm, LANES), lambda i, j: (i, 0))],
        out_shape=[_sds((n, PROJ_W), F32), _sds((n, LANES), F32)],
        scratch_shapes=[pltpu.VMEM((tm, D_MODEL), BF16)],
        compiler_params=_cparams(("parallel", "arbitrary")),
        name="inproj",
    )(x2, g, w_main, w_f, b_f, colgain, *tabs)


def _lower_tri(n, strict=False):
    r = lax.broadcasted_iota(I32, (n, n), 0)
    c = lax.broadcasted_iota(I32, (n, n), 1)
    return (c < r) if strict else (c <= r)


def _cumsum_body(x_ref, o_ref, carry):
    @pl.when(pl.program_id(1) == 0)
    def _():
        carry[...] = jnp.zeros_like(carry)

    x = x_ref[0]
    tc = x.shape[0]
    tri = _lower_tri(tc).astype(F32)
    cs = jnp.dot(tri, x, precision=HIGHEST, preferred_element_type=F32) + carry[...]
    o_ref[0] = cs
    carry[...] = cs[tc - 1:tc, :]


def _cumsum(logf3, *, tc):
    b, t, w = logf3.shape
    return pl.pallas_call(
        _cumsum_body,
        grid=(b, t // tc),
        in_specs=[pl.BlockSpec((1, tc, w), lambda i, j: (i, j, 0))],
        out_specs=pl.BlockSpec((1, tc, w), lambda i, j: (i, j, 0)),
        out_shape=_sds((b, t, w), F32),
        scratch_shapes=[pltpu.VMEM((1, w), F32)],
        compiler_params=_cparams(("parallel", "arbitrary")),
        name="logf_cumsum",
    )(logf3)


def _dot_exact01(x, m01):
    hi = x.astype(BF16)
    r1 = x - hi.astype(F32)
    mid = r1.astype(BF16)
    lo = (r1 - mid.astype(F32)).astype(BF16)
    return (jnp.dot(hi, m01, preferred_element_type=F32) + jnp.dot(mid, m01, preferred_element_type=F32)
            + jnp.dot(lo, m01, preferred_element_type=F32))


def _paged_cumsum_body(pt_ref, *refs, pages):
    del pt_ref
    page_refs = refs[:pages]
    new_ref, cum_rep_ref, cum_ref = refs[pages:pages + 3]
    o_ref, onew_ref, carry = refs[pages + 3:]
    s = pl.program_id(1)

    @pl.when(s == 0)
    def _():
        carry[...] = jnp.zeros_like(carry)

    x = jnp.concatenate([r[0] for r in page_refs], axis=0)
    w = _dot_exact01(x, cum_rep_ref[...]).reshape(pages, N_HEADS, PAGE_ROWS)
    tot = w[:, :, PAGE_ROWS - 1:PAGE_ROWS]
    inc = tot
    sh = 1
    while sh < pages:
        inc = inc + jnp.concatenate([jnp.zeros((sh, N_HEADS, 1), F32), inc[:pages - sh]], axis=0)
        sh *= 2
    offs = jnp.concatenate([jnp.zeros((1, N_HEADS, 1), F32), inc[:pages - 1]], axis=0) + carry[...]
    for p in range(pages):
        o_ref[0, :, p * PAGE_ROWS:(p + 1) * PAGE_ROWS] = -(w[p] + offs[p])
    c = carry[...] + inc[pages - 1]
    carry[...] = c

    @pl.when(s == pl.num_programs(1) - 1)
    def _():
        onew_ref[0] = _dot_exact01(new_ref[0], cum_ref[...]) + c


def _paged_cumsum(pt_flat, cache_logf_t, logf_new_t, *, n_pages, pages):
    b = logf_new_t.shape[0]
    upper = np.triu(np.ones((PAGE, PAGE), np.float32))
    cum = jnp.asarray(upper, BF16)
    cum_rep = jnp.asarray(np.repeat(upper, N_KV, axis=1), BF16)

    def page_map(p):
        return lambda i, s, pt: (pt[i * n_pages + s * pages + p], 0, 0)

    const = lambda i, s, pt: (0, 0)
    grid_spec = pltpu.PrefetchScalarGridSpec(
        num_scalar_prefetch=1,
        grid=(b, n_pages // pages),
        in_specs=[pl.BlockSpec((1, N_HEADS, PAGE), page_map(p)) for p in range(pages)]
        + [pl.BlockSpec((1, N_HEADS, LANES), lambda i, s, pt: (i, 0, 0)),
           pl.BlockSpec((PAGE, PAGE_ROWS), const), pl.BlockSpec((PAGE, PAGE), const)],
        out_specs=[pl.BlockSpec((1, N_HEADS, pages * PAGE_ROWS), lambda i, s, pt: (i, 0, s)),
                   pl.BlockSpec((1, N_HEADS, LANES), lambda i, s, pt: (i, 0, 0))],
        scratch_shapes=[pltpu.VMEM((N_HEADS, 1), F32)],
    )
    return pl.pallas_call(
        functools.partial(_paged_cumsum_body, pages=pages),
        grid_spec=grid_spec,
        out_shape=[_sds((b, N_HEADS, n_pages * PAGE_ROWS), F32), _sds((b, N_HEADS, LANES), F32)],
        compiler_params=_cparams(("parallel", "arbitrary")),
        name="paged_logf_cumsum",
    )(pt_flat, *([cache_logf_t] * pages), logf_new_t, cum_rep, cum)


def _softmax_tile(t, rowshift, v, carry):
    m, l, acc = carry
    m_new = jnp.maximum(m, jnp.max(t, axis=-1, keepdims=True) + rowshift)
    p = jnp.exp(t + (rowshift - m_new))
    alpha = jnp.exp(m - m_new)
    l = alpha * l + jnp.sum(p, axis=-1, keepdims=True)
    acc = alpha * acc + jnp.dot(p.astype(BF16), v, preferred_element_type=F32)
    return m_new, l, acc


def _softmax_init(rows):
    return (jnp.full((rows, 1), NEG, F32), jnp.zeros((rows, 1), F32), jnp.zeros((rows, HEAD_DIM), F32))


def _qk(q, k):
    return lax.dot_general(q, k, (((1,), (1,)), ((), ())), preferred_element_type=F32)


def _fox_body(q_ref, k_ref, v_ref, cq_ref, ck_ref, o_ref, *, tq):
    h = pl.program_id(1)
    qi = pl.program_id(2)
    lane = lax.broadcasted_iota(I32, (tq, LANES), 1)
    qs, cqs = [], []
    for g in range(GROUP):
        qs.append((q_ref[0, :, g * HEAD_DIM:(g + 1) * HEAD_DIM] * ATTN_SCALE).astype(BF16))
        cqs.append(jnp.sum(jnp.where(lane == h * GROUP + g, cq_ref[0], 0.0), axis=-1, keepdims=True))

    def tile(start, diag, carries):
        k = k_ref[0, pl.ds(start, tq), :].astype(BF16)
        v = v_ref[0, pl.ds(start, tq), :].astype(BF16)
        out = []
        for g in range(GROUP):
            ck = ck_ref[0, pl.ds(h * GROUP + g, 1), pl.ds(start, tq)]
            t = _qk(qs[g], k) - ck
            if diag:
                t = jnp.where(_lower_tri(tq), t, NEG)
            out.append(_softmax_tile(t, cqs[g], v, carries[g]))
        return tuple(out)

    carries = tile(pl.multiple_of(qi * tq, tq), True, (_softmax_init(tq),) * GROUP)
    carries = lax.fori_loop(0, qi, lambda j, c: tile(pl.multiple_of(j * tq, tq), False, c), carries)
    for g in range(GROUP):
        _, l, acc = carries[g]
        o_ref[0, :, g * HEAD_DIM:(g + 1) * HEAD_DIM] = (acc / l).astype(o_ref.dtype)


def _fox_prompt(proj3, c3, ck_t, *, tq):
    b, t, _ = proj3.shape
    gw = GROUP * HEAD_DIM
    qb, kb, vb = COL_FOX_Q // gw, COL_FOX_K // HEAD_DIM, COL_FOX_V // HEAD_DIM
    return pl.pallas_call(
        functools.partial(_fox_body, tq=tq),
        grid=(b, N_KV, t // tq),
        in_specs=[
            pl.BlockSpec((1, tq, gw), lambda i, h, q: (i, q, qb + h)),
            pl.BlockSpec((1, t, HEAD_DIM), lambda i, h, q: (i, 0, kb + h)),
            pl.BlockSpec((1, t, HEAD_DIM), lambda i, h, q: (i, 0, vb + h)),
            pl.BlockSpec((1, tq, LANES), lambda i, h, q: (i, q, 0)),
            pl.BlockSpec((1, N_HEADS, t), lambda i, h, q: (i, 0, 0)),
        ],
        out_specs=pl.BlockSpec((1, tq, gw), lambda i, h, q: (i, q, h)),
        out_shape=_sds((b, t, Q_W), BF16),
        compiler_params=_cparams(("parallel", "parallel", "arbitrary")),
        name="fox_prompt",
    )(proj3, proj3, proj3, c3, ck_t)


def _topk_select(g, elig, n_cand, axis):
    idx = lax.broadcasted_iota(I32, g.shape, axis)
    g = jnp.where(elig, g, -jnp.inf)
    cnt = jnp.zeros(g.shape, F32)
    for m in range(n_cand):
        gm = g[m:m + 1, :] if axis == 0 else g[:, m:m + 1]
        beats = (gm > g) | ((gm == g) & (m < idx))
        cnt = cnt + jnp.where(beats, 1.0, 0.0)
    return elig & (cnt < MOBA_TOPK)


def _moba_body(q_ref, qall_ref, k_ref, v_ref, o_ref, kmean_scr, sel_scr, *, n_blk):
    qi = pl.program_id(2)
    bs = MOBA_BLOCK
    tq = 2 * bs
    nb8 = kmean_scr.shape[0]
    t_all = qall_ref.shape[1]

    @pl.when(qi == 0)
    def _():
        kmean_scr[...] = jnp.zeros_like(kmean_scr)
        for n in range(n_blk):
            kb = k_ref[0, n * bs:(n + 1) * bs, :]
            kmean_scr[n:n + 1, :] = jnp.mean(kb, axis=0, keepdims=True)
        eye = jnp.where(lax.broadcasted_iota(I32, (bs, bs), 0) == lax.broadcasted_iota(I32, (bs, bs), 1),
                        1.0, 0.0).astype(BF16)
        blk = lax.broadcasted_iota(I32, (nb8, t_all), 0)
        q_blk = lax.shift_right_logical(lax.broadcasted_iota(I32, (nb8, t_all), 1), bs.bit_length() - 1)
        for g in range(GROUP):
            gate_t = lax.dot_general(kmean_scr[...], qall_ref[0, :, g * HEAD_DIM:(g + 1) * HEAD_DIM],
                                     (((1,), (1,)), ((), ())), precision=HIGHEST, preferred_element_type=F32)
            sel_t = jnp.where(_topk_select(gate_t, blk < q_blk, n_blk, 0), 1.0, 0.0).astype(BF16)
            for c in range(t_all // bs):
                sel_scr[g, c * bs:(c + 1) * bs, :] = _qk(eye, sel_t[:, c * bs:(c + 1) * bs])

    lane_blk = lax.broadcasted_iota(I32, (tq, nb8), 1)
    qs, sels = [], []
    for g in range(GROUP):
        qs.append((q_ref[0, :, g * HEAD_DIM:(g + 1) * HEAD_DIM] * ATTN_SCALE).astype(BF16))
        sels.append(sel_scr[g, pl.ds(pl.multiple_of(qi * tq, tq), tq), :])

    def load(j):
        start = pl.multiple_of(j * tq, tq)
        return (k_ref[0, pl.ds(start, tq), :].astype(BF16), v_ref[0, pl.ds(start, tq), :].astype(BF16))

    def row_bias(g, n):
        flag = jnp.sum(jnp.where(lane_blk == n, sels[g], 0.0), axis=-1, keepdims=True)
        return (flag - 1.0) * (-NEG)

    col_first = lax.broadcasted_iota(I32, (tq, tq), 1) < bs
    row_second = lax.broadcasted_iota(I32, (tq, tq), 0) >= bs

    def past_pair(j, carries):
        k, v = load(j)
        out = []
        for g in range(GROUP):
            bias = jnp.where(col_first, row_bias(g, 2 * j), row_bias(g, 2 * j + 1))
            out.append(_softmax_tile(_qk(qs[g], k) + bias, 0.0, v, carries[g]))
        return tuple(out)

    k, v = load(qi)
    carries = []
    for g in range(GROUP):
        t = jnp.where(_lower_tri(tq), _qk(qs[g], k), NEG)
        t = t + jnp.where(col_first & row_second, row_bias(g, 2 * qi), 0.0)
        carries.append(_softmax_tile(t, 0.0, v, _softmax_init(tq)))
    carries = lax.fori_loop(0, qi, past_pair, tuple(carries))
    for g in range(GROUP):
        _, l, acc = carries[g]
        o_ref[0, :, g * HEAD_DIM:(g + 1) * HEAD_DIM] = (acc / l).astype(o_ref.dtype)


def _moba_prompt(proj3):
    b, t, _ = proj3.shape
    n_blk = t // MOBA_BLOCK
    assert t % (2 * MOBA_BLOCK) == 0
    nb8 = -(-n_blk // SUBLANES) * SUBLANES
    gw = GROUP * HEAD_DIM
    qb, kb, vb = COL_MOBA_Q // gw, COL_MOBA_K // HEAD_DIM, COL_MOBA_V // HEAD_DIM
    tq = 2 * MOBA_BLOCK
    return pl.pallas_call(
        functools.partial(_moba_body, n_blk=n_blk),
        grid=(b, N_KV, t // tq),
        in_specs=[
            pl.BlockSpec((1, tq, gw), lambda i, h, q: (i, q, qb + h)),
            pl.BlockSpec((1, t, gw), lambda i, h, q: (i, 0, qb + h)),
            pl.BlockSpec((1, t, HEAD_DIM), lambda i, h, q: (i, 0, kb + h)),
            pl.BlockSpec((1, t, HEAD_DIM), lambda i, h, q: (i, 0, vb + h)),
        ],
        out_specs=pl.BlockSpec((1, tq, gw), lambda i, h, q: (i, q, h)),
        out_shape=_sds((b, t, Q_W), BF16),
        scratch_shapes=[pltpu.VMEM((nb8, HEAD_DIM), F32), pltpu.VMEM((GROUP, t, nb8), F32)],
        compiler_params=_cparams(("parallel", "parallel", "arbitrary")),
        name="moba_prompt",
    )(proj3, proj3, proj3, proj3)


def _fox_decode_body(pt_ref, *refs, pages, rows):
    del pt_ref
    q_ref, cq_ref, nck_ref, mask_ref, knew_ref, vnew_ref, bnew_ref = refs[:7]
    k_refs = refs[7:7 + pages]
    v_refs = refs[7 + pages:7 + 2 * pages]
    o_ref, m_scr, l_scr, acc_scr = refs[7 + 2 * pages:]
    s = pl.program_id(1)

    @pl.when(s == 0)
    def _():
        m_scr[...] = jnp.full_like(m_scr, NEG)
        l_scr[...] = jnp.zeros_like(l_scr)
        acc_scr[...] = jnp.zeros_like(acc_scr)

    q = (q_ref[0] * ATTN_SCALE).astype(BF16)
    cq = cq_ref[0]
    m, l, acc = m_scr[...], l_scr[...], acc_scr[...]
    t = jnp.concatenate([_qk(q, k_refs[p][0].astype(BF16)) for p in range(pages)], axis=1)
    t = t + jnp.concatenate([nck_ref[0]] * (rows // N_HEADS), axis=0) + mask_ref[...]
    m_new = jnp.maximum(m, jnp.max(t, axis=-1, keepdims=True) + cq)
    pf = jnp.exp(t + (cq - m_new))
    alpha = jnp.exp(m - m_new)
    l = alpha * l + jnp.sum(pf, axis=-1, keepdims=True)
    p_ = pf.astype(BF16)
    pv = jnp.dot(p_[:, 0:PAGE_ROWS], v_refs[0][0].astype(BF16), preferred_element_type=F32)
    for p in range(1, pages):
        pv = pv + jnp.dot(p_[:, p * PAGE_ROWS:(p + 1) * PAGE_ROWS], v_refs[p][0].astype(BF16),
                          preferred_element_type=F32)
    acc = alpha * acc + pv
    m_scr[...], l_scr[...], acc_scr[...] = m_new, l, acc

    @pl.when(s == pl.num_programs(1) - 1)
    def _():
        tn = _qk(q, knew_ref[0].astype(BF16)) + bnew_ref[0]
        _, l2, acc2 = _softmax_tile(tn, cq, vnew_ref[0].astype(BF16), (m_new, l, acc))
        o_ref[0] = (acc2 / l2).astype(o_ref.dtype)


def _fox_decode(pt_flat, q32, cq32, nck_rep, mask, k_new, v_new, bias_new, cache_k, cache_v,
                *, n_pages, pages):
    b, rows, _ = q32.shape
    n_new = k_new.shape[1]

    def page_map(p):
        return lambda i, s, pt: (pt[i * n_pages + s * pages + p], 0, 0)

    per_b = lambda i, s, pt: (i, 0, 0)
    grid_spec = pltpu.PrefetchScalarGridSpec(
        num_scalar_prefetch=1,
        grid=(b, n_pages // pages),
        in_specs=[
            pl.BlockSpec((1, rows, HEAD_DIM), per_b),
            pl.BlockSpec((1, rows, 1), per_b),
            pl.BlockSpec((1, N_HEADS, pages * PAGE_ROWS), lambda i, s, pt: (i, 0, s)),
            pl.BlockSpec((rows, pages * PAGE_ROWS), lambda i, s, pt: (0, 0)),
            pl.BlockSpec((1, n_new, HEAD_DIM), per_b),
            pl.BlockSpec((1, n_new, HEAD_DIM), per_b),
            pl.BlockSpec((1, rows, n_new), per_b),
        ] + [pl.BlockSpec((1, PAGE_ROWS, HEAD_DIM), page_map(p)) for p in range(pages)] * 2,
        out_specs=pl.BlockSpec((1, rows, HEAD_DIM), per_b),
        scratch_shapes=[pltpu.VMEM((rows, 1), F32), pltpu.VMEM((rows, 1), F32),
                        pltpu.VMEM((rows, HEAD_DIM), F32)],
    )
    return pl.pallas_call(
        functools.partial(_fox_decode_body, pages=pages, rows=rows),
        grid_spec=grid_spec,
        out_shape=_sds((b, rows, HEAD_DIM), BF16),
        compiler_params=_cparams(("parallel", "arbitrary")),
        name="fox_decode",
    )(pt_flat, q32, cq32, nck_rep, jnp.tile(mask, (1, pages)), k_new, v_new, bias_new,
      *([cache_k] * pages), *([cache_v] * pages))


def _tree_sum(x):
    while x.shape[0] > 1:
        half = x.shape[0] // 2
        x = x[:half] + x[half:]
    return x[0]


def _moba_decode_body(pt_ref, *refs, nb, n_past, rows):
    del pt_ref
    ppb = MOBA_BLOCK // PAGE
    pages = nb * ppb
    q_ref, mask_ref, knew_ref, vnew_ref, bnew_ref = refs[:5]
    k_refs = refs[5:5 + pages]
    v_refs = refs[5 + pages:5 + 2 * pages]
    o_ref, m_scr, l_scr, g_scr, acc_scr = refs[5 + 2 * pages:]
    s = pl.program_id(1)

    @pl.when(s == 0)
    def _():
        m_scr[...] = jnp.zeros_like(m_scr)
        l_scr[...] = jnp.zeros_like(l_scr)
        g_scr[...] = jnp.zeros_like(g_scr)

    qf = q_ref[0]
    q = (qf * ATTN_SCALE).astype(BF16)
    lane = lax.broadcasted_iota(I32, (rows, LANES), 1)
    m_all, l_all, g_all = m_scr[...], l_scr[...], g_scr[...]
    for j in range(nb):
        n = s * nb + j
        kf = jnp.concatenate([k_refs[j * ppb + p][0] for p in range(ppb)], axis=0)
        vf = jnp.concatenate([v_refs[j * ppb + p][0] for p in range(ppb)], axis=0)
        t = _qk(q, kf.astype(BF16)) + mask_ref[...]
        m_n = jnp.max(t, axis=-1, keepdims=True)
        p_ = jnp.exp(t - m_n)
        l_n = jnp.sum(p_, axis=-1, keepdims=True)
        acc_scr[n] = jnp.dot(p_.astype(BF16), vf.astype(BF16), preferred_element_type=F32)
        s8 = _tree_sum(kf.reshape(MOBA_BLOCK * N_KV // SUBLANES, SUBLANES, HEAD_DIM))
        ks = s8[0:N_KV] + s8[N_KV:2 * N_KV]
        ks_h = jnp.concatenate([ks[h // GROUP:h // GROUP + 1] for h in range(N_HEADS)], axis=0)
        ks_r = jnp.concatenate([ks_h] * (rows // N_HEADS), axis=0)
        g_n = jnp.sum(qf * ks_r, axis=-1, keepdims=True) * (1.0 / MOBA_BLOCK)
        m_all = jnp.where(lane == n, m_n, m_all)
        l_all = jnp.where(lane == n, l_n, l_all)
        g_all = jnp.where(lane == n, g_n, g_all)
    m_scr[...], l_scr[...], g_scr[...] = m_all, l_all, g_all

    @pl.when(s == pl.num_programs(1) - 1)
    def _():
        t = _qk(q, knew_ref[0].astype(BF16)) + bnew_ref[...]
        m_own = jnp.max(t, axis=-1, keepdims=True)
        p_own = jnp.exp(t - m_own)
        l_own = jnp.sum(p_own, axis=-1, keepdims=True)
        acc_own = jnp.dot(p_own.astype(BF16), vnew_ref[0].astype(BF16), preferred_element_type=F32)
        sel = _topk_select(g_all, lane < n_past, n_past, 1)
        m_x = jnp.maximum(m_own, jnp.max(jnp.where(sel, m_all, NEG), axis=-1, keepdims=True))
        w = jnp.where(sel, jnp.exp(m_all - m_x), 0.0)
        w_own = jnp.exp(m_own - m_x)
        den = w_own * l_own + jnp.sum(w * l_all, axis=-1, keepdims=True)
        num = w_own * acc_own
        for n in range(n_past):
            num = num + w[:, n:n + 1] * acc_scr[n]
        o_ref[0] = (num / den).astype(o_ref.dtype)


def _moba_decode(pt_flat, q32, mask, k_new, v_new, bias_new, cache_k, cache_v, *, n_pages, nb):
    b, rows, _ = q32.shape
    ppb = MOBA_BLOCK // PAGE
    pages = nb * ppb
    n_past = n_pages // ppb
    assert n_past <= LANES and n_past % nb == 0
    n_new = k_new.shape[1]

    def page_map(p):
        return lambda i, s, pt: (pt[i * n_pages + s * pages + p], 0, 0)

    per_b = lambda i, s, pt: (i, 0, 0)
    grid_spec = pltpu.PrefetchScalarGridSpec(
        num_scalar_prefetch=1,
        grid=(b, n_past // nb),
        in_specs=[
            pl.BlockSpec((1, rows, HEAD_DIM), per_b),
            pl.BlockSpec((rows, ppb * PAGE_ROWS), lambda i, s, pt: (0, 0)),
            pl.BlockSpec((1, n_new, HEAD_DIM), per_b),
            pl.BlockSpec((1, n_new, HEAD_DIM), per_b),
            pl.BlockSpec((rows, n_new), lambda i, s, pt: (0, 0)),
        ] + [pl.BlockSpec((1, PAGE_ROWS, HEAD_DIM), page_map(p)) for p in range(pages)] * 2,
        out_specs=pl.BlockSpec((1, rows, HEAD_DIM), per_b),
        scratch_shapes=[pltpu.VMEM((rows, LANES), F32), pltpu.VMEM((rows, LANES), F32),
                        pltpu.VMEM((rows, LANES), F32), pltpu.VMEM((n_past, rows, HEAD_DIM), F32)],
    )
    return pl.pallas_call(
        functools.partial(_moba_decode_body, nb=nb, n_past=n_past, rows=rows),
        grid_spec=grid_spec,
        out_shape=_sds((b, rows, HEAD_DIM), BF16),
        compiler_params=_cparams(("parallel", "arbitrary")),
        name="moba_decode",
    )(pt_flat, q32, mask, k_new, v_new, bias_new, *([cache_k] * pages), *([cache_v] * pages))


def _outproj_body(oa_ref, ob_ref, ga_ref, gb_ref, x_ref, wa_ref, wb_ref, wo_ref, o_ref):
    ya = jnp.dot(oa_ref[...], wa_ref[...], preferred_element_type=F32)
    yb = jnp.dot(ob_ref[...], wb_ref[...], preferred_element_type=F32)
    mixed = jax.nn.sigmoid(ga_ref[...]) * ya + jax.nn.sigmoid(gb_ref[...]) * yb
    o_ref[...] = x_ref[...] + jnp.dot(mixed.astype(BF16), wo_ref[...], preferred_element_type=F32)


def _outproj(oa, ob, proj, x2, wa, wb, wo, *, tm):
    n = x2.shape[0]
    const = lambda i: (0, 0)
    once = pl.Buffered(1)
    return pl.pallas_call(
        _outproj_body,
        grid=(n // tm,),
        in_specs=[
            pl.BlockSpec((tm, Q_W), lambda i: (i, 0)),
            pl.BlockSpec((tm, Q_W), lambda i: (i, 0)),
            pl.BlockSpec((tm, D_MODEL), lambda i: (i, COL_GA // D_MODEL)),
            pl.BlockSpec((tm, D_MODEL), lambda i: (i, COL_GB // D_MODEL)),
            pl.BlockSpec((tm, D_MODEL), lambda i: (i, 0)),
            pl.BlockSpec((Q_W, D_MODEL), const, pipeline_mode=once),
            pl.BlockSpec((Q_W, D_MODEL), const, pipeline_mode=once),
            pl.BlockSpec((D_MODEL, D_MODEL), const, pipeline_mode=once),
        ],
        out_specs=pl.BlockSpec((tm, D_MODEL), lambda i: (i, 0)),
        out_shape=_sds((n, D_MODEL), F32),
        compiler_params=_cparams(("parallel",)),
        name="outproj",
    )(oa, ob, proj, proj, x2, wa, wb, wo)


def _first_lane(hit, lane):
    return jnp.min(jnp.where(hit, lane.astype(F32), float(LANES)), axis=-1, keepdims=True).astype(I32)


def _router_body(x_ref, g_ref, wr_ref, hn_ref, ii_ref, iw_ref, cnt_ref, cnt_scr):
    @pl.when(pl.program_id(0) == 0)
    def _():
        cnt_scr[...] = jnp.zeros_like(cnt_scr)

    x = x_ref[...]
    tm = x.shape[0]
    ms = jnp.mean(x * x, axis=-1, keepdims=True)
    hn = x * lax.rsqrt(ms + RMS_EPS) * g_ref[...]
    hn_ref[...] = hn
    h_hi = hn.astype(BF16)
    h_lo = (hn - h_hi.astype(F32)).astype(BF16)
    quad = jnp.dot(jnp.concatenate([h_hi, h_lo], axis=0), wr_ref[...], preferred_element_type=F32)
    logits = (quad[:tm, :LANES] + quad[:tm, LANES:]) + (quad[tm:, :LANES] + quad[tm:, LANES:])
    lane = lax.broadcasted_iota(I32, (tm, LANES), 1)
    is_g = lane < N_GROUPS
    gl = jnp.where(is_g, logits, -jnp.inf)
    gmax = jnp.max(gl, axis=-1, keepdims=True)
    grp = _first_lane(gl == gmax, lane)
    p_grp = 1.0 / jnp.sum(jnp.where(is_g, jnp.exp(gl - gmax), 0.0), axis=-1, keepdims=True)
    e_lane = lane - N_GROUPS
    shift = EXPERTS_PER_GROUP.bit_length() - 1
    in_grp = (e_lane < N_EXPERTS) & (lax.shift_right_arithmetic(e_lane, shift) == grp)
    el = jnp.where(in_grp, logits, -jnp.inf)
    v1 = jnp.max(el, axis=-1, keepdims=True)
    i1 = _first_lane(el == v1, lane)
    el2 = jnp.where(lane == i1, -jnp.inf, el)
    v2 = jnp.max(el2, axis=-1, keepdims=True)
    i2 = _first_lane(el2 == v2, lane)
    r = jnp.exp(v2 - v1)
    w1 = p_grp / (1.0 + r)
    w2 = p_grp * r / (1.0 + r)
    e1 = i1 - N_GROUPS
    e2 = i2 - N_GROUPS
    oh = jnp.where((lane == e1) | (lane == e2), 1.0, 0.0)
    before = cnt_scr[...] + jnp.dot(_lower_tri(tm, strict=True).astype(BF16), oh.astype(BF16),
                                    preferred_element_type=F32)
    r1 = jnp.sum(jnp.where(lane == e1, before, 0.0), axis=-1, keepdims=True)
    r2 = jnp.sum(jnp.where(lane == e2, before, 0.0), axis=-1, keepdims=True)
    cnt_scr[...] = cnt_scr[...] + jnp.sum(oh, axis=0, keepdims=True)
    cnt_ref[...] = cnt_scr[...].astype(I32)
    ii = jnp.where(lane == 0, e1, jnp.where(lane == 1, e2, jnp.where(
        lane == 2, r1.astype(I32), jnp.where(lane == 3, r2.astype(I32), 0))))
    ii_ref[...] = ii
    iw_ref[...] = jnp.where(lane == 0, w1, jnp.where(lane == 1, w2, 0.0))


def _router(x1, g, wr2, *, tm):
    n = x1.shape[0]
    return pl.pallas_call(
        _router_body,
        grid=(n // tm,),
        in_specs=[pl.BlockSpec((tm, D_MODEL), lambda i: (i, 0)),
                  pl.BlockSpec((1, D_MODEL), lambda i: (0, 0)),
                  pl.BlockSpec((D_MODEL, 2 * LANES), lambda i: (0, 0))],
        out_specs=[pl.BlockSpec((tm, D_MODEL), lambda i: (i, 0)),
                   pl.BlockSpec((tm, LANES), lambda i: (i, 0)),
                   pl.BlockSpec((tm, LANES), lambda i: (i, 0)),
                   pl.BlockSpec((1, LANES), lambda i: (0, 0))],
        out_shape=[_sds((n, D_MODEL), F32), _sds((n, LANES), I32), _sds((n, LANES), F32),
                   _sds((1, LANES), I32)],
        scratch_shapes=[pltpu.VMEM((1, LANES), F32)],
        compiler_params=_cparams(("arbitrary",)),
        name="moe_router",
    )(x1, g, wr2)


def _row_token_body(dest_ref, tok_ref, *, n_tok, n_rows):
    def zero(i, c):
        tok_ref[i] = 0
        return c

    lax.fori_loop(0, n_rows, zero, 0, unroll=8)

    def put(t, c):
        tok_ref[dest_ref[t]] = t
        tok_ref[dest_ref[n_tok + t]] = t
        return c

    lax.fori_loop(0, n_tok, put, 0, unroll=8)


def _row_token(dest_flat, *, n_rows):
    n_tok = dest_flat.shape[0] // 2
    return pl.pallas_call(
        functools.partial(_row_token_body, n_tok=n_tok, n_rows=n_rows),
        in_specs=[pl.BlockSpec(memory_space=pltpu.SMEM)],
        out_specs=pl.BlockSpec(memory_space=pltpu.SMEM),
        out_shape=_sds((n_rows,), I32),
        name="moe_row_token",
    )(dest_flat)


GATHER_UNROLL = 8


def _row_gather_start(src_hbm, idx_ref, base, buf, row0, sem, n):
    def body(r, c):
        pltpu.make_async_copy(src_hbm.at[pl.ds(idx_ref[base + r], 1)], buf.at[pl.ds(row0 + r, 1)],
                              sem).start()
        return c

    lax.fori_loop(0, n, body, 0, unroll=GATHER_UNROLL)


def _row_gather_wait(src_hbm, buf, sem):
    pltpu.make_async_copy(src_hbm.at[pl.ds(0, buf.shape[0])], buf, sem).wait()


def _experts_body(be_ref, nu_ref, tok_ref, hn_hbm, wg_ref, wu_ref, wd_ref, y_ref,
                  xbuf, wg_s, wu_s, wd_s, sems, *, tb):
    i = pl.program_id(0)
    n_used = nu_ref[0]
    slot = i % 2

    @pl.when(i == 0)
    def _():
        _row_gather_start(hn_hbm, tok_ref, 0, xbuf.at[0], 0, sems.at[0], tb)

    @pl.when(i + 1 < n_used)
    def _():
        _row_gather_start(hn_hbm, tok_ref, (i + 1) * tb, xbuf.at[1 - slot], 0, sems.at[1 - slot], tb)

    new_expert = (i == 0) | (be_ref[i] != be_ref[jnp.maximum(i - 1, 0)])

    @pl.when((i < n_used) & new_expert)
    def _():
        wg_s[...] = wg_ref[0].astype(BF16)
        wu_s[...] = wu_ref[0].astype(BF16)
        wd_s[...] = wd_ref[0].astype(BF16)

    @pl.when(i < n_used)
    def _():
        _row_gather_wait(hn_hbm, xbuf.at[slot], sems.at[slot])
        x = xbuf[slot].astype(BF16)
        hg = jnp.dot(x, wg_s[...], preferred_element_type=F32)
        hu = jnp.dot(x, wu_s[...], preferred_element_type=F32)
        hid = (hg * jax.nn.sigmoid(hg) * hu).astype(BF16)
        y_ref[...] = jnp.dot(hid, wd_s[...], preferred_element_type=F32)

    @pl.when(i >= n_used)
    def _():
        y_ref[...] = jnp.zeros_like(y_ref)


def _experts(blk_expert, n_used, row_tok, hn, wg, wu, wd, *, tb):
    n_blocks = blk_expert.shape[0]
    wmap = lambda i, be, nu, tok: (be[i], 0, 0)
    grid_spec = pltpu.PrefetchScalarGridSpec(
        num_scalar_prefetch=3,
        grid=(n_blocks,),
        in_specs=[pl.BlockSpec(memory_space=pl.ANY),
                  pl.BlockSpec((1, D_MODEL, D_EXPERT), wmap),
                  pl.BlockSpec((1, D_MODEL, D_EXPERT), wmap),
                  pl.BlockSpec((1, D_EXPERT, D_MODEL), wmap)],
        out_specs=pl.BlockSpec((tb, D_MODEL), lambda i, be, nu, tok: (i, 0)),
        scratch_shapes=[pltpu.VMEM((2, tb, D_MODEL), F32),
                        pltpu.VMEM((D_MODEL, D_EXPERT), BF16), pltpu.VMEM((D_MODEL, D_EXPERT), BF16),
                        pltpu.VMEM((D_EXPERT, D_MODEL), BF16), pltpu.SemaphoreType.DMA((2,))],
    )
    return pl.pallas_call(
        functools.partial(_experts_body, tb=tb),
        grid_spec=grid_spec,
        out_shape=_sds((n_blocks * tb, D_MODEL), F32),
        compiler_params=_cparams(("arbitrary",)),
        name="moe_experts",
    )(blk_expert, n_used, row_tok, hn, wg, wu, wd)


def _combine_body(dest_ref, x_ref, iw_ref, y_hbm, o_ref, ybuf, sems, *, tm, n):
    i = pl.program_id(0)
    n_steps = pl.num_programs(0)
    slot = i % 2

    def start(step, to_slot):
        for k in range(2):
            _row_gather_start(y_hbm, dest_ref, k * n + step * tm, ybuf.at[to_slot], k * tm,
                              sems.at[to_slot], tm)

    @pl.when(i == 0)
    def _():
        start(0, 0)

    @pl.when(i + 1 < n_steps)
    def _():
        start(i + 1, 1 - slot)

    _row_gather_wait(y_hbm, ybuf.at[slot], sems.at[slot])
    iw = iw_ref[...]
    o_ref[...] = (x_ref[...] + iw[:, 0:1] * ybuf[slot, 0:tm, :] + iw[:, 1:2] * ybuf[slot, tm:2 * tm, :])


def _combine(dest_flat, x1, iw, y, *, tm):
    n = x1.shape[0]
    grid_spec = pltpu.PrefetchScalarGridSpec(
        num_scalar_prefetch=1,
        grid=(n // tm,),
        in_specs=[pl.BlockSpec((tm, D_MODEL), lambda i, d: (i, 0)),
                  pl.BlockSpec((tm, LANES), lambda i, d: (i, 0)),
                  pl.BlockSpec(memory_space=pl.ANY)],
        out_specs=pl.BlockSpec((tm, D_MODEL), lambda i, d: (i, 0)),
        scratch_shapes=[pltpu.VMEM((2, 2 * tm, D_MODEL), F32), pltpu.SemaphoreType.DMA((2,))],
    )
    return pl.pallas_call(
        functools.partial(_combine_body, tm=tm, n=n),
        grid_spec=grid_spec,
        out_shape=_sds((n, D_MODEL), F32),
        compiler_params=_cparams(("arbitrary",)),
        name="moe_combine",
    )(dest_flat, x1, iw, y)


def _moe(x1, norm_g, wr2, wg, wu, wd, *, tm, tb):
    n = x1.shape[0]
    hn, ii, iw, cnt = _router(x1, norm_g, wr2, tm=tm)
    counts = cnt[0, :N_EXPERTS]
    padded = (counts + tb - 1) // tb * tb
    pend = jnp.cumsum(padded)
    pstart = pend - padded
    n_blocks = -(-2 * n // tb) + N_EXPERTS
    eids = jnp.arange(N_EXPERTS, dtype=I32)
    e12 = ii[:, 0:2]
    start12 = jnp.sum(jnp.where(e12[:, :, None] == eids, pstart, 0), axis=-1)
    dest_flat = (start12 + ii[:, 2:4]).T.reshape(-1).astype(I32)
    blk_start = jnp.arange(n_blocks, dtype=I32) * tb
    blk_expert = jnp.minimum(jnp.sum(blk_start[:, None] >= pend[None, :], axis=-1), N_EXPERTS - 1).astype(I32)
    n_used = (pend[-1:] // tb).astype(I32)
    row_tok = _row_token(dest_flat, n_rows=n_blocks * tb)
    y = _experts(blk_expert, n_used, row_tok, hn, wg, wu, wd, tb=tb)
    return _combine(dest_flat, x1, iw, y, tm=tm)


def _rope_tables(pos):
    half = ROT_DIM // 2
    inv = jnp.power(ROPE_THETA, -jnp.arange(0, ROT_DIM, 2, dtype=F32) / ROT_DIM)
    ang = pos.astype(F32)[:, None] * inv[None, :]
    cos, sin = jnp.cos(ang), jnp.sin(ang)
    t = pos.shape[0]
    rest = HEAD_DIM - ROT_DIM
    cos_t = jnp.concatenate([cos, cos, jnp.ones((t, rest), F32)], axis=1)
    sa = jnp.concatenate([jnp.zeros((t, half), F32), sin, jnp.zeros((t, rest), F32)], axis=1)
    sb = jnp.concatenate([-sin, jnp.zeros((t, half + rest), F32)], axis=1)
    return cos_t, sa, sb


def _head_match_mask(rows, cols):
    r = np.arange(rows)[:, None]
    c = np.arange(cols)[None, :]
    return np.where((c % N_KV) == ((r % N_HEADS) // GROUP), 0.0, NEG).astype(np.float32)


def _prep_weights(w_in, b_forget, gains, w_branch_fox, w_branch_moba, w_out, w_router_group,
                  w_router_expert):
    fq, fk, mq, mk = gains
    off_f = Q_W + 2 * KV_W
    w_main = jnp.concatenate([w_in[:, :off_f], w_in[:, off_f + N_HEADS:]], axis=1).astype(BF16)
    w_f = jnp.pad(w_in[:, off_f:off_f + N_HEADS], ((0, 0), (0, LANES - N_HEADS))).astype(BF16)
    b_f = jnp.pad(b_forget, (0, LANES - N_HEADS))[None, :]
    ones = lambda w: jnp.ones((w,), F32)
    colgain = jnp.concatenate([
        jnp.tile(fq, N_HEADS), jnp.tile(fk, N_KV), ones(KV_W),
        jnp.tile(mq, N_HEADS), jnp.tile(mk, N_KV), ones(KV_W), ones(2 * D_MODEL)])[None, :]
    wr = jnp.pad(jnp.concatenate([w_router_group, w_router_expert], axis=1),
                 ((0, 0), (0, LANES - N_GROUPS - N_EXPERTS)))
    wr_hi = wr.astype(BF16)
    wr_lo = (wr - wr_hi.astype(F32)).astype(BF16)
    return dict(w_main=w_main, w_f=w_f, b_f=b_f, colgain=colgain,
                wa=w_branch_fox.astype(BF16), wb=w_branch_moba.astype(BF16), wo=w_out.astype(BF16),
                wr2=jnp.concatenate([wr_hi, wr_lo], axis=1))


def _pick(n, pref):
    t = min(n, pref)
    while n % t:
        t //= 2
    return t


def _prompt_layer(x, wts, experts, norm_mix, norm_ffn):
    b, t, _ = x.shape
    n = b * t
    x2 = x.reshape(n, D_MODEL)
    tm = _pick(t, 1024)
    tabs = _rope_tables(jnp.arange(t, dtype=I32))
    proj, logf = _inproj(x2, norm_mix[None, :], wts["w_main"], wts["w_f"], wts["b_f"], wts["colgain"],
                         tabs, tm=tm)
    proj3 = proj.reshape(b, t, PROJ_W)
    c3 = _cumsum(logf.reshape(b, t, LANES), tc=_pick(t, 256))
    ck_t = jnp.transpose(c3[:, :, :N_HEADS], (0, 2, 1))
    oa = _fox_prompt(proj3, c3, ck_t, tq=_pick(t, 512))
    ob = _moba_prompt(proj3)
    x1 = _outproj(oa.reshape(n, Q_W), ob.reshape(n, Q_W), proj, x2, wts["wa"], wts["wb"], wts["wo"],
                  tm=_pick(n, 256))
    y = _moe(x1, norm_ffn[None, :], wts["wr2"], *experts, tm=_pick(n, 256), tb=256)
    return y.reshape(b, t, D_MODEL), proj3, logf.reshape(b, t, LANES)[:, :, :N_HEADS]


def _decode_layer(x, wts, experts, norm_mix, norm_ffn, caches, page_table):
    b, t, _ = x.shape
    n = b * t
    n_pages = page_table.shape[1]
    past = n_pages * PAGE
    assert past % MOBA_BLOCK == 0 and t <= MOBA_BLOCK and (t * N_HEADS) % SUBLANES == 0
    cache_fk, cache_fv, cache_logf, cache_mk, cache_mv = caches
    n_pool = cache_fk.shape[0]
    x2 = x.reshape(n, D_MODEL)
    pos = past + jnp.arange(t, dtype=I32)
    tabs = tuple(jnp.tile(tb_, (b, 1)) for tb_ in _rope_tables(pos))
    proj, logf = _inproj(x2, norm_mix[None, :], wts["w_main"], wts["w_f"], wts["b_f"], wts["colgain"],
                         tabs, tm=n)
    pt_flat = page_table.reshape(-1).astype(I32)
    rows = t * N_HEADS
    logf_bt = logf.reshape(b, t, LANES)[:, :, :N_HEADS]
    logf_new_t = jnp.pad(jnp.transpose(logf_bt, (0, 2, 1)), ((0, 0), (0, 0), (0, LANES - t)))
    nck_rep, c_new_t = _paged_cumsum(pt_flat, jnp.transpose(cache_logf, (0, 2, 1)), logf_new_t,
                                     n_pages=n_pages, pages=_pick(n_pages, 64))
    c_new_t = c_new_t[:, :, :t]
    cq32 = jnp.transpose(c_new_t, (0, 2, 1)).reshape(b, rows, 1)
    r_t = np.arange(rows)[:, None] // N_HEADS
    c_t = np.arange(t * N_KV)[None, :] // N_KV
    new_mask = _head_match_mask(rows, t * N_KV) + np.where(c_t <= r_t, 0.0, NEG).astype(np.float32)
    bias_new = jnp.tile(jnp.repeat(-c_new_t, N_KV, axis=-1), (1, t, 1)) + new_mask[None]
    page_mask = jnp.asarray(_head_match_mask(rows, PAGE_ROWS))
    blk_mask = jnp.asarray(_head_match_mask(rows, MOBA_BLOCK * N_KV))

    def q_rows(col):
        return proj[:, col:col + Q_W].reshape(b, rows, HEAD_DIM)

    def kv_rows(col):
        return proj[:, col:col + KV_W].reshape(b, t * N_KV, HEAD_DIM)

    page_rows = lambda c: c.reshape(n_pool, PAGE_ROWS, HEAD_DIM)
    oa = _fox_decode(pt_flat, q_rows(COL_FOX_Q), cq32, nck_rep, page_mask, kv_rows(COL_FOX_K),
                     kv_rows(COL_FOX_V), bias_new, page_rows(cache_fk), page_rows(cache_fv),
                     n_pages=n_pages, pages=_pick(n_pages, 32))
    ob = _moba_decode(pt_flat, q_rows(COL_MOBA_Q), blk_mask, kv_rows(COL_MOBA_K), kv_rows(COL_MOBA_V),
                      jnp.asarray(new_mask), page_rows(cache_mk), page_rows(cache_mv),
                      n_pages=n_pages, nb=_pick(n_pages * PAGE // MOBA_BLOCK, 16))
    x1 = _outproj(oa.reshape(n, Q_W), ob.reshape(n, Q_W), proj, x2, wts["wa"], wts["wb"], wts["wo"], tm=n)
    y = _moe(x1, norm_ffn[None, :], wts["wr2"], *experts, tm=n, tb=16)
    return y.reshape(b, t, D_MODEL), proj.reshape(b, t, PROJ_W), logf_bt


KV_COLS = (COL_FOX_K, COL_FOX_V, COL_MOBA_K, COL_MOBA_V)


def _kv_pack_body(*refs):
    srcs, dsts = refs[:len(KV_COLS)], refs[len(KV_COLS):]
    tm = srcs[0].shape[0]
    for src, dst in zip(srcs, dsts):
        for hh in range(N_KV):
            dst[pl.ds(hh, tm, stride=N_KV), :] = src[:, hh * HEAD_DIM:(hh + 1) * HEAD_DIM]


def _kv_pack(proj, *, tm):
    n = proj.shape[0]
    return pl.pallas_call(
        _kv_pack_body,
        grid=(n // tm,),
        in_specs=[pl.BlockSpec((tm, KV_W), lambda i, c=c: (i, c // KV_W)) for c in KV_COLS],
        out_specs=[pl.BlockSpec((tm * N_KV, HEAD_DIM), lambda i: (i, 0))] * len(KV_COLS),
        out_shape=[_sds((n * N_KV, HEAD_DIM), F32)] * len(KV_COLS),
        compiler_params=_cparams(("parallel",)),
        name="kv_pack",
    )(*([proj] * len(KV_COLS)))


def _kv_outputs(proj3, logf):
    b, t, _ = proj3.shape
    fk, fv, mk, mv = (a.reshape(1, b, t, N_KV, HEAD_DIM)
                      for a in _kv_pack(proj3.reshape(b * t, PROJ_W), tm=_pick(b * t, 512)))
    return fk, fv, logf[None], mk, mv


def kernel(x_prompt, x_sample, cache_fox_k, cache_fox_v, cache_fox_logf, cache_moba_k, cache_moba_v,
           page_table, norm_mix, w_in, b_forget, fox_q_norm, fox_k_norm, moba_q_norm, moba_k_norm,
           w_branch_fox, w_branch_moba, w_out, norm_ffn, w_router_group, w_router_expert,
           w_exp_gate, w_exp_up, w_exp_down):
    assert w_in.shape[0] == 1, "single-layer trunk"
    wts = _prep_weights(w_in[0], b_forget[0], (fox_q_norm[0], fox_k_norm[0], moba_q_norm[0], moba_k_norm[0]),
                        w_branch_fox[0], w_branch_moba[0], w_out[0], w_router_group[0], w_router_expert[0])
    experts = (w_exp_gate[0], w_exp_up[0], w_exp_down[0])
    caches = (cache_fox_k[0], cache_fox_v[0], cache_fox_logf[0], cache_moba_k[0], cache_moba_v[0])
    ys, proj_s, logf_s = _decode_layer(x_sample, wts, experts, norm_mix[0], norm_ffn[0], caches, page_table)
    yp, proj_p, logf_p = _prompt_layer(x_prompt, wts, experts, norm_mix[0], norm_ffn[0])
    return (yp, ys) + _kv_outputs(proj_p, logf_p) + _kv_outputs(proj_s, logf_s)
```
